```python
import math
import jax, jax.numpy as jnp
from jax import lax
import numpy as np

D_MODEL = 1024
BATCH = 8
SEQ = 4096
DEPTH = 4

CHUNK = 64
QBLK = 128
HEAD_DIM = 64
H_SB = 4
H_FOX = 4
H_DIFF = 4
W_SB = H_SB * HEAD_DIM
W_FOX = H_FOX * HEAD_DIM
W_DIFF_QK = H_DIFF * 2 * HEAD_DIM
DIFF_V_DIM = 2 * HEAD_DIM
W_DIFF = H_DIFF * DIFF_V_DIM
MIX_WIDTH = W_SB + W_FOX + W_DIFF
N_BRANCH = 3
ROT_DIM = HEAD_DIM // 4
ROPE_THETA = 500000.0
D_FF = 2816
P_DIM = 256
EPS = 1e-6
FORGET_BIAS_INIT = 2.0
IN_SIZES = (W_SB, W_SB, W_SB, W_FOX, W_FOX, W_FOX, H_FOX, W_DIFF_QK, W_DIFF_QK, W_DIFF, N_BRANCH * D_MODEL)
IN_COLS = 3 * W_SB + 3 * W_FOX + H_FOX + 2 * W_DIFF_QK + W_DIFF + N_BRANCH * D_MODEL

kernel_name = 'hybrid_sb_fox_diff_macaron_ple'


def rms_norm(x, gain):
    xf = x.astype(jnp.float32)
    y = xf * lax.rsqrt(jnp.mean(xf * xf, axis=-1, keepdims=True) + EPS)
    return (y * gain.astype(jnp.float32)).astype(x.dtype)


def swiglu_ffn(h, gain, wi, wo):
    a, g = jnp.split(rms_norm(h, gain) @ wi, 2, axis=-1)
    return (jax.nn.silu(a) * g) @ wo


def rope_tables(positions):
    inv_freq = ROPE_THETA ** (-jnp.arange(0, ROT_DIM, 2, dtype=jnp.float32) / ROT_DIM)
    ang = positions.astype(jnp.float32)[..., None] * inv_freq
    return jnp.cos(ang), jnp.sin(ang)


def apply_partial_rope(x, cos, sin):
    c = cos[:, :, None, None, :]
    s = sin[:, :, None, None, :]
    x1 = x[..., :ROT_DIM // 2].astype(jnp.float32)
    x2 = x[..., ROT_DIM // 2:ROT_DIM].astype(jnp.float32)
    rot = jnp.concatenate([x1 * c - x2 * s, x2 * c + x1 * s], axis=-1).astype(x.dtype)
    return jnp.concatenate([rot, x[..., ROT_DIM:]], axis=-1)


def to_heads(x, n_heads, d):
    b, s, _ = x.shape
    return x.reshape(b, s, n_heads, d).transpose(0, 2, 1, 3)


def from_heads(o):
    b, h, s, d = o.shape
    return o.transpose(0, 2, 1, 3).reshape(b, s, h * d)


def block_indices(i):
    lo, hi = i * QBLK, (i + 1) * QBLK
    t_idx = lo + jnp.arange(QBLK)[:, None]
    s_idx = jnp.arange(hi)[None, :]
    return lo, hi, t_idx, s_idx


def stick_breaking_attention(q, k, v):
    scale = HEAD_DIM ** -0.5
    outs = []
    for i in range(q.shape[2] // QBLK):
        lo, hi, t_idx, s_idx = block_indices(i)
        strict = s_idx < t_idx
        z = jnp.einsum('bhqd,bhkd->bhqk', q[:, :, lo:hi], k[:, :, :hi]).astype(jnp.float32) * scale
        log_beta = jax.nn.log_sigmoid(z)
        log_keep = jnp.where(strict, jax.nn.log_sigmoid(-z), 0.0)
        tail = lax.cumsum(log_keep, axis=3, reverse=True) - log_keep
        a = jnp.where(strict, jnp.exp(log_beta + tail), 0.0)
        outs.append(jnp.einsum('bhqk,bhkd->bhqd', a.astype(v.dtype), v[:, :, :hi]))
    return jnp.concatenate(outs, axis=2)


def forgetting_attention(q, k, v, log_f):
    scale = HEAD_DIM ** -0.5
    cum = jnp.cumsum(log_f, axis=-1)
    outs = []
    for i in range(q.shape[2] // QBLK):
        lo, hi, t_idx, s_idx = block_indices(i)
        logits = jnp.einsum('bhqd,bhkd->bhqk', q[:, :, lo:hi], k[:, :, :hi]).astype(jnp.float32) * scale
        logits = logits + cum[:, :, lo:hi, None] - cum[:, :, None, :hi]
        probs = jax.nn.softmax(jnp.where(s_idx <= t_idx, logits, -jnp.inf), axis=-1)
        outs.append(jnp.einsum('bhqk,bhkd->bhqd', probs.astype(v.dtype), v[:, :, :hi]))
    return jnp.concatenate(outs, axis=2)


def differential_attention(q1, q2, k1, k2, v, lam):
    scale = HEAD_DIM ** -0.5
    outs = []
    for i in range(q1.shape[2] // QBLK):
        lo, hi, t_idx, s_idx = block_indices(i)
        mask = (s_idx // CHUNK) <= (t_idx // CHUNK)
        s1 = jnp.einsum('bhqd,bhkd->bhqk', q1[:, :, lo:hi], k1[:, :, :hi]).astype(jnp.float32) * scale
        s2 = jnp.einsum('bhqd,bhkd->bhqk', q2[:, :, lo:hi], k2[:, :, :hi]).astype(jnp.float32) * scale
        a1 = jax.nn.softmax(jnp.where(mask, s1, -jnp.inf), axis=-1)
        a2 = jax.nn.softmax(jnp.where(mask, s2, -jnp.inf), axis=-1)
        w = a1 - lam * a2
        outs.append(jnp.einsum('bhqk,bhkd->bhqd', w.astype(v.dtype), v[:, :, :hi]))
    return jnp.concatenate(outs, axis=2)


def token_mixing(u, cos, sin, w_in, b_forget, qk_gain_fox, qk_gain_diff, diff_lambda, diff_subln, w_br, w_o, lam_init):
    b, s, _ = u.shape
    split_idx = np.cumsum(IN_SIZES)[:-1].tolist()
    qa, ka, va, qb, kb, vb, fb, qc, kc, vc, gates = jnp.split(u @ w_in, split_idx, axis=-1)

    oa = stick_breaking_attention(to_heads(qa, H_SB, HEAD_DIM), to_heads(ka, H_SB, HEAD_DIM), to_heads(va, H_SB, HEAD_DIM))

    qb = rms_norm(qb.reshape(b, s, H_FOX, HEAD_DIM), qk_gain_fox[0]).transpose(0, 2, 1, 3)
    kb = rms_norm(kb.reshape(b, s, H_FOX, HEAD_DIM), qk_gain_fox[1]).transpose(0, 2, 1, 3)
    log_f = jax.nn.log_sigmoid(fb.astype(jnp.float32) + b_forget.astype(jnp.float32)).transpose(0, 2, 1)
    ob = forgetting_attention(qb, kb, to_heads(vb, H_FOX, HEAD_DIM), log_f)

    qc = apply_partial_rope(rms_norm(qc.reshape(b, s, H_DIFF, 2, HEAD_DIM), qk_gain_diff[0]), cos, sin)
    kc = apply_partial_rope(rms_norm(kc.reshape(b, s, H_DIFF, 2, HEAD_DIM), qk_gain_diff[1]), cos, sin)
    q1, q2 = qc[:, :, :, 0].transpose(0, 2, 1, 3), qc[:, :, :, 1].transpose(0, 2, 1, 3)
    k1, k2 = kc[:, :, :, 0].transpose(0, 2, 1, 3), kc[:, :, :, 1].transpose(0, 2, 1, 3)
    lf = diff_lambda.astype(jnp.float32)
    lam = jnp.exp(jnp.sum(lf[0] * lf[1])) - jnp.exp(jnp.sum(lf[2] * lf[3])) + lam_init
    oc = differential_attention(q1, q2, k1, k2, to_heads(vc, H_DIFF, DIFF_V_DIM), lam)
    oc = rms_norm(oc, diff_subln) * (1.0 - lam_init)

    y_a = from_heads(oa) @ w_br[:W_SB]
    y_b = from_heads(ob) @ w_br[W_SB:W_SB + W_FOX]
    y_c = from_heads(oc) @ w_br[W_SB + W_FOX:]
    g = jax.nn.sigmoid(gates.reshape(b, s, N_BRANCH, D_MODEL))
    merged = g[:, :, 0] * y_a + g[:, :, 1] * y_b + g[:, :, 2] * y_c
    return merged @ w_o


def setup_inputs(seed: int = 0) -> dict:
    key = jax.random.key(seed)
    ks = jax.random.split(key, 24)

    def w(k, shape, fan_in, gain=1.0):
        return jax.random.normal(k, shape, jnp.float32) * (gain * fan_in ** -0.5)

    def gn(k, shape):
        return 1.0 + 0.05 * jax.random.normal(k, shape, jnp.float32)

    x = jax.random.normal(ks[0], (BATCH, SEQ, D_MODEL), jnp.float32)
    p = jax.random.normal(ks[1], (DEPTH, BATCH, SEQ, P_DIM), jnp.float32)
    start = jax.random.randint(ks[2], (BATCH, 1), 0, 8192, dtype=jnp.int32)
    positions = start + jnp.arange(SEQ, dtype=jnp.int32)[None, :]
    return {
        'x': x,
        'p': p,
        'positions': positions,
        'ffn1_norm': gn(ks[3], (DEPTH, D_MODEL)),
        'ffn1_wi': w(ks[4], (DEPTH, D_MODEL, 2 * D_FF), D_MODEL),
        'ffn1_wo': w(ks[5], (DEPTH, D_FF, D_MODEL), D_FF, 0.5),
        'mix_norm': gn(ks[6], (DEPTH, D_MODEL)),
        'w_in': w(ks[7], (DEPTH, D_MODEL, IN_COLS), D_MODEL),
        'b_forget': FORGET_BIAS_INIT + 0.1 * jax.random.normal(ks[8], (DEPTH, H_FOX), jnp.float32),
        'qk_gain_fox': gn(ks[9], (DEPTH, 2, HEAD_DIM)),
        'qk_gain_diff': gn(ks[10], (DEPTH, 2, HEAD_DIM)),
        'diff_lambda': 0.1 * jax.random.normal(ks[11], (DEPTH, 4, HEAD_DIM), jnp.float32),
        'diff_subln': gn(ks[12], (DEPTH, DIFF_V_DIM)),
        'w_br': w(ks[13], (DEPTH, MIX_WIDTH, D_MODEL), MIX_WIDTH),
        'w_o': w(ks[14], (DEPTH, D_MODEL, D_MODEL), D_MODEL, 0.5),
        'ffn2_norm': gn(ks[15], (DEPTH, D_MODEL)),
        'ffn2_wi': w(ks[16], (DEPTH, D_MODEL, 2 * D_FF), D_MODEL),
        'ffn2_wo': w(ks[17], (DEPTH, D_FF, D_MODEL), D_FF, 0.5),
        'ple_norm': gn(ks[18], (DEPTH, D_MODEL)),
        'ple_gate_w': w(ks[19], (DEPTH, D_MODEL, D_MODEL), D_MODEL),
        'ple_proj_w': w(ks[20], (DEPTH, P_DIM, D_MODEL), P_DIM, 0.5),
    }


def reference(x, p, positions, ffn1_norm, ffn1_wi, ffn1_wo, mix_norm, w_in, b_forget, qk_gain_fox, qk_gain_diff, diff_lambda, diff_subln, w_br, w_o, ffn2_norm, ffn2_wi, ffn2_wo, ple_norm, ple_gate_w, ple_proj_w):
    cos, sin = rope_tables(positions)
    h = x
    for i in range(DEPTH):
        lam_init = 0.8 - 0.6 * math.exp(-0.3 * i)
        h = h + 0.5 * swiglu_ffn(h, ffn1_norm[i], ffn1_wi[i], ffn1_wo[i])
        u = rms_norm(h, mix_norm[i])
        h = h + token_mixing(u, cos, sin, w_in[i], b_forget[i], qk_gain_fox[i], qk_gain_diff[i], diff_lambda[i], diff_subln[i], w_br[i], w_o[i], lam_init)
        h = h + 0.5 * swiglu_ffn(h, ffn2_norm[i], ffn2_wi[i], ffn2_wo[i])
        gate = jax.nn.sigmoid(rms_norm(h, ple_norm[i]) @ ple_gate_w[i])
        h = h + gate * (p[i] @ ple_proj_w[i])
    return h
```

```python
import functools
import math

import jax
import jax.numpy as jnp
from jax import lax
from jax.experimental import pallas as pl
from jax.experimental.pallas import tpu as pltpu

D_MODEL = 1024
HEAD_DIM = 64
LANES = 128
N_PAIR_SB = 2
N_PAIR_FOX = 2
H_DIFF = 4
W_SB = 256
W_FOX = 256
W_DIFF = 512
ROT_DIM = 16
ROPE_THETA = 500000.0
D_FF = 2816
FF_CHUNK = 256
P_DIM = 256
EPS = 1e-6
CHUNK = 64
SCALE = HEAD_DIM ** -0.5
IN_OFFSETS = {"a": 0, "b": 768, "fb": 1536, "c": 1540, "gates": 3076, "end": 6148}
VMEM_LIMIT = 56 * 1024 * 1024

F32 = jnp.float32
BF16 = jnp.bfloat16
NT_DIMS = (((1,), (1,)), ((), ()))


def _dot(a, b):
    return jnp.dot(a, b, preferred_element_type=F32)


def _dot_nt(a, b):
    return lax.dot_general(a, b, NT_DIMS, preferred_element_type=F32)


def _rms(x, gain):
    ms = jnp.mean(x * x, axis=-1, keepdims=True)
    return x * lax.rsqrt(ms + EPS) * gain


def _split_bf16(x, parts):
    out = []
    for _ in range(parts - 1):
        piece = x.astype(BF16)
        out.append(piece)
        x = x - piece.astype(F32)
    out.append(x.astype(BF16))
    return out


def _params(n_grid):
    return pltpu.CompilerParams(
        dimension_semantics=("arbitrary",) * n_grid, vmem_limit_bytes=VMEM_LIMIT)


def _const_spec(shape):
    zeros = (0,) * len(shape)
    return pl.BlockSpec(shape, lambda *_: zeros, pipeline_mode=pl.Buffered(1))


def _ffn_kernel(h_ref, gain_ref, wa_ref, wg_ref, wo_ref, o_ref, acc_ref):
    h = h_ref[...]
    xn = _rms(h, gain_ref[...]).astype(BF16)
    acc_ref[...] = jnp.zeros_like(acc_ref)

    def chunk(c, carry):
        a = _dot(xn, wa_ref[c])
        g = _dot(xn, wg_ref[c])
        act = (a * jax.nn.sigmoid(a) * g).astype(BF16)
        acc_ref[...] += _dot(act, wo_ref[c])
        return carry

    lax.fori_loop(0, wa_ref.shape[0], chunk, 0)
    o_ref[...] = h + 0.5 * acc_ref[...]


def _ffn(h, gain, wa, wg, wo, tm):
    t = h.shape[0]
    row = pl.BlockSpec((tm, D_MODEL), lambda i: (i, 0))
    return pl.pallas_call(
        _ffn_kernel,
        grid=(t // tm,),
        in_specs=[row, _const_spec(gain.shape), _const_spec(wa.shape), _const_spec(wg.shape),
                  _const_spec(wo.shape)],
        out_specs=row,
        out_shape=jax.ShapeDtypeStruct(h.shape, F32),
        scratch_shapes=[pltpu.VMEM((tm, D_MODEL), F32)],
        compiler_params=_params(1),
        name="ffn",
    )(h, gain, wa, wg, wo)


def _rope_kernel(pos_ref, const_ref, c_ref, s1_ref, s2_ref):
    ang = pos_ref[...].astype(F32) * const_ref[0:1, :]
    sin = jnp.sin(ang)
    c_ref[...] = jnp.cos(ang)
    s1_ref[...] = sin * const_ref[1:2, :]
    s2_ref[...] = sin * const_ref[2:3, :]


def _rope_tables(positions, tm):
    t = positions.size
    lane = jnp.arange(LANES) % HEAD_DIM
    inv_freq = ROPE_THETA ** (-jnp.arange(0, ROT_DIM, 2, dtype=F32) / ROT_DIM)
    half = ROT_DIM // 2
    freq_lane = jnp.where(lane < ROT_DIM, inv_freq[lane % half], 0.0)
    plus = jnp.where((lane >= half) & (lane < ROT_DIM), 1.0, 0.0)
    minus = jnp.where(lane < half, -1.0, 0.0)
    const = jnp.zeros((8, LANES), F32).at[0].set(freq_lane).at[1].set(plus).at[2].set(minus)
    out = jax.ShapeDtypeStruct((t, LANES), F32)
    row = pl.BlockSpec((tm, LANES), lambda i: (i, 0))
    return pl.pallas_call(
        _rope_kernel,
        grid=(t // tm,),
        in_specs=[pl.BlockSpec((tm, 1), lambda i: (i, 0)), _const_spec(const.shape)],
        out_specs=[row, row, row],
        out_shape=[out, out, out],
        compiler_params=_params(1),
        name="rope_tables",
    )(positions.reshape(t, 1), const)


def _inproj_kernel(h_ref, gain_ref, w_ref, gfox_ref, gdiff_ref, grp_ref, rc_ref, rs1_ref, rs2_ref,
                   ya_ref, yb_ref, yc_ref, fb_ref):
    xn = _rms(h_ref[...], gain_ref[...]).astype(BF16)
    grp = grp_ref[...]

    def head_norm(x, gain):
        ms = _dot((x * x).astype(BF16), grp)
        return x * lax.rsqrt(ms + EPS) * gain

    def rope(x):
        return (x * rc_ref[...] + pltpu.roll(x, ROT_DIM // 2, 1) * rs1_ref[...]
                + pltpu.roll(x, LANES - ROT_DIM // 2, 1) * rs2_ref[...])

    y = _dot(xn, w_ref[:, 0:768])
    ya_ref[:, 0:W_SB] = (y[:, 0:W_SB] * SCALE).astype(BF16)
    ya_ref[:, W_SB:768] = y[:, W_SB:768].astype(BF16)

    y = _dot(xn, w_ref[:, 768:1536])
    for blk in range(2 * N_PAIR_FOX):
        is_q = blk < N_PAIR_FOX
        x = head_norm(y[:, blk * LANES:(blk + 1) * LANES], gfox_ref[(0 if is_q else 1):(1 if is_q else 2), :])
        if is_q:
            x = x * SCALE
        yb_ref[:, blk * LANES:(blk + 1) * LANES] = x.astype(BF16)
    yb_ref[:, 2 * W_FOX:768] = y[:, 2 * W_FOX:768].astype(BF16)

    y = _dot(xn, w_ref[:, 1536:3072])
    for blk in range(2 * H_DIFF):
        is_q = blk < H_DIFF
        x = head_norm(y[:, blk * LANES:(blk + 1) * LANES], gdiff_ref[(0 if is_q else 1):(1 if is_q else 2), :])
        x = rope(x)
        if is_q:
            x = x * SCALE
        yc_ref[:, blk * LANES:(blk + 1) * LANES] = x.astype(BF16)
    yc_ref[:, 2 * W_DIFF:1536] = y[:, 2 * W_DIFF:1536].astype(BF16)

    fb_ref[...] = _dot(xn, w_ref[:, 3072:3200])


def _inproj(h, gain, w, gfox, gdiff, grp, rc, rs1, rs2, tm):
    t = h.shape[0]
    row = lambda n: pl.BlockSpec((tm, n), lambda i: (i, 0))
    return pl.pallas_call(
        _inproj_kernel,
        grid=(t // tm,),
        in_specs=[row(D_MODEL), _const_spec(gain.shape), _const_spec(w.shape), _const_spec(gfox.shape),
                  _const_spec(gdiff.shape), _const_spec(grp.shape), row(LANES), row(LANES), row(LANES)],
        out_specs=[row(768), row(768), row(1536), row(LANES)],
        out_shape=[jax.ShapeDtypeStruct((t, 768), BF16), jax.ShapeDtypeStruct((t, 768), BF16),
                   jax.ShapeDtypeStruct((t, 1536), BF16), jax.ShapeDtypeStruct((t, LANES), F32)],
        compiler_params=_params(1),
        name="inproj",
    )(h, gain, w, gfox, gdiff, grp, rc, rs1, rs2)


def _forget_kernel(fb_ref, bias_ref, tri_ref, o_ref):
    s = fb_ref.shape[1]
    tri = tri_ref[...]
    carry = jnp.zeros((16, 1), F32)
    for blk in range(s // LANES):
        x = fb_ref[0, blk * LANES:(blk + 1) * LANES, :] + bias_ref[...]
        log_f = jnp.minimum(x, 0.0) - jnp.log(1.0 + jnp.exp(-jnp.abs(x)))
        lt = log_f.T[0:16, :]
        cum = carry
        for piece in _split_bf16(lt, 3):
            cum = cum + _dot(piece, tri)
        o_ref[0, 0, :, blk * LANES:(blk + 1) * LANES] = -cum[0:8, :]
        o_ref[0, 1, :, blk * LANES:(blk + 1) * LANES] = -cum[8:16, :]
        carry = cum[:, LANES - 1:LANES]


def _forget_cumsum(fb, bias, tri):
    b, s, _ = fb.shape
    return pl.pallas_call(
        _forget_kernel,
        grid=(b,),
        in_specs=[pl.BlockSpec((1, s, LANES), lambda i: (i, 0, 0)), _const_spec(bias.shape),
                  _const_spec(tri.shape)],
        out_specs=pl.BlockSpec((1, 2, 8, s), lambda i: (i, 0, 0, 0)),
        out_shape=jax.ShapeDtypeStruct((b, 2, 8, s), F32),
        compiler_params=_params(1),
        name="forget_cumsum",
    )(fb, bias, tri)


def _merge_kernel(h_ref, oa_ref, ob_ref, oc_ref, gain_ref, wgate_ref, wbr_ref, wo_ref, o_ref):
    h = h_ref[...]
    u = _rms(h, gain_ref[...]).astype(BF16)
    merged = None
    row0 = 0
    for br, o_br in enumerate((oa_ref, ob_ref, oc_ref)):
        width = o_br.shape[1]
        y = _dot(o_br[...], wbr_ref[row0:row0 + width, :])
        gate = jax.nn.sigmoid(_dot(u, wgate_ref[:, br * D_MODEL:(br + 1) * D_MODEL]))
        merged = gate * y if merged is None else merged + gate * y
        row0 += width
    o_ref[...] = h + _dot(merged.astype(BF16), wo_ref[...])


def _merge(h, oa, ob, oc, gain, wgate, wbr, wo, tm):
    t = h.shape[0]
    row = lambda n: pl.BlockSpec((tm, n), lambda i: (i, 0))
    return pl.pallas_call(
        _merge_kernel,
        grid=(t // tm,),
        in_specs=[row(D_MODEL), row(W_SB), row(W_FOX), row(W_DIFF), _const_spec(gain.shape),
                  _const_spec(wgate.shape), _const_spec(wbr.shape), _const_spec(wo.shape)],
        out_specs=row(D_MODEL),
        out_shape=jax.ShapeDtypeStruct(h.shape, F32),
        compiler_params=_params(1),
        name="merge",
    )(h, oa, ob, oc, gain, wgate, wbr, wo)


def _ple_kernel(h_ref, p_ref, gain_ref, wgate_ref, wproj_ref, o_ref):
    h = h_ref[...]
    u = _rms(h, gain_ref[...]).astype(BF16)
    gate = jax.nn.sigmoid(_dot(u, wgate_ref[...]))
    o_ref[...] = h + gate * _dot(p_ref[...].astype(BF16), wproj_ref[...])


def _ple(h, p, gain, wgate, wproj, tm):
    t = h.shape[0]
    row = lambda n: pl.BlockSpec((tm, n), lambda i: (i, 0))
    return pl.pallas_call(
        _ple_kernel,
        grid=(t // tm,),
        in_specs=[row(D_MODEL), row(P_DIM), _const_spec(gain.shape), _const_spec(wgate.shape),
                  _const_spec(wproj.shape)],
        out_specs=row(D_MODEL),
        out_shape=jax.ShapeDtypeStruct(h.shape, F32),
        compiler_params=_params(1),
        name="ple",
    )(h, p, gain, wgate, wproj)


def _low_lanes():
    return lax.broadcasted_iota(jnp.int32, (1, LANES), 1) < HEAD_DIM


def _split_heads(q):
    low = _low_lanes()
    zero = jnp.zeros_like(q)
    return jnp.where(low, q, zero), jnp.where(low, zero, q)


def _attn_sb_kernel(q_ref, k_ref, v_ref, uo_ref, o_ref, *, tq):
    i = pl.program_id(2)
    qs = _split_heads(q_ref[0])
    row = lax.broadcasted_iota(jnp.int32, (tq, tq), 0)
    col = lax.broadcasted_iota(jnp.int32, (tq, tq), 1)
    strict = col < row
    uo = uo_ref[...]

    def block(kb, carry, diag):
        start = pl.multiple_of(kb * tq, tq)
        k = k_ref[0, pl.ds(start, tq), :]
        v = v_ref[0, pl.ds(start, tq), :]
        out = []
        for h in range(2):
            run, acc = carry[h]
            z = _dot_nt(qs[h], k)
            log_beta = jnp.minimum(z, 0.0) - jnp.log(1.0 + jnp.exp(-jnp.abs(z)))
            log_keep = log_beta - z
            if diag:
                log_keep = jnp.where(strict, log_keep, 0.0)
            sums = None
            for piece in _split_bf16(log_keep, 2):
                part = _dot(piece, uo)
                sums = part if sums is None else sums + part
            a = jnp.exp(log_beta + sums[:, :tq] + run)
            if diag:
                a = jnp.where(strict, a, 0.0)
            out.append((run + sums[:, tq:], acc + _dot(a.astype(BF16), v)))
        return tuple(out)

    zero_run = jnp.zeros((tq, tq), F32)
    zero_acc = jnp.zeros((tq, LANES), F32)
    carry = block(i, ((zero_run, zero_acc), (zero_run, zero_acc)), True)
    carry = lax.fori_loop(0, i, lambda n, c: block(i - 1 - n, c, False), carry)
    o_ref[0] = jnp.where(_low_lanes(), carry[0][1], carry[1][1]).astype(o_ref.dtype)


def _attn_sb(ya, uo, tq):
    b, s, _ = ya.shape
    return pl.pallas_call(
        functools.partial(_attn_sb_kernel, tq=tq),
        grid=(b, N_PAIR_SB, s // tq),
        in_specs=[pl.BlockSpec((1, tq, LANES), lambda bi, j, i: (bi, i, j)),
                  pl.BlockSpec((1, s, LANES), lambda bi, j, i: (bi, 0, N_PAIR_SB + j)),
                  pl.BlockSpec((1, s, LANES), lambda bi, j, i: (bi, 0, 2 * N_PAIR_SB + j)),
                  _const_spec(uo.shape)],
        out_specs=pl.BlockSpec((1, tq, LANES), lambda bi, j, i: (bi, i, j)),
        out_shape=jax.ShapeDtypeStruct((b, s, W_SB), BF16),
        compiler_params=_params(3),
        name="attn_stickbreak",
    )(ya, ya, ya, uo)


def _softmax_step(s, state, v):
    m, acc = state
    m_new = jnp.maximum(m, jnp.max(s, axis=1, keepdims=True))
    p = jnp.exp(s - m_new)
    return m_new, acc * jnp.exp(m - m_new) + _dot(p.astype(BF16), v)


def _attn_fox_kernel(q_ref, k_ref, v_ref, nc_ref, o_ref, vones_ref, *, tq):
    i = pl.program_id(2)
    low = _low_lanes()

    @pl.when(i == 0)
    def _():
        v = v_ref[0]
        one = jnp.ones_like(v)
        vones_ref[0] = jnp.where(low, v, one)
        vones_ref[1] = jnp.where(low, one, v)

    qs = _split_heads(q_ref[0])
    row = lax.broadcasted_iota(jnp.int32, (tq, tq), 0)
    col = lax.broadcasted_iota(jnp.int32, (tq, tq), 1)
    causal = col <= row

    def block(kb, carry, diag):
        start = pl.multiple_of(kb * tq, tq)
        k = k_ref[0, pl.ds(start, tq), :]
        out = []
        for h in range(2):
            s = _dot_nt(qs[h], k) + nc_ref[0, 0, h:h + 1, pl.ds(start, tq)]
            if diag:
                s = jnp.where(causal, s, -jnp.inf)
            out.append(_softmax_step(s, carry[h], vones_ref[h, pl.ds(start, tq), :]))
        return tuple(out)

    init = (jnp.full((tq, 1), -jnp.inf, F32), jnp.zeros((tq, LANES), F32))
    carry = block(i, (init, init), True)
    carry = lax.fori_loop(0, i, lambda n, c: block(i - 1 - n, c, False), carry)
    outs = [acc / pltpu.roll(acc, HEAD_DIM, 1) for _, acc in carry]
    o_ref[0] = jnp.where(low, outs[0], outs[1]).astype(o_ref.dtype)


def _attn_fox(yb, negcum, tq):
    b, s, _ = yb.shape
    return pl.pallas_call(
        functools.partial(_attn_fox_kernel, tq=tq),
        grid=(b, N_PAIR_FOX, s // tq),
        in_specs=[pl.BlockSpec((1, tq, LANES), lambda bi, j, i: (bi, i, j)),
                  pl.BlockSpec((1, s, LANES), lambda bi, j, i: (bi, 0, N_PAIR_FOX + j)),
                  pl.BlockSpec((1, s, LANES), lambda bi, j, i: (bi, 0, 2 * N_PAIR_FOX + j)),
                  pl.BlockSpec((1, 1, 8, s), lambda bi, j, i: (bi, j, 0, 0))],
        out_specs=pl.BlockSpec((1, tq, LANES), lambda bi, j, i: (bi, i, j)),
        out_shape=jax.ShapeDtypeStruct((b, s, W_FOX), BF16),
        scratch_shapes=[pltpu.VMEM((2, s, LANES), BF16)],
        compiler_params=_params(3),
        name="attn_forget",
    )(yb, yb, yb, negcum)


def _attn_diff_kernel(q_ref, k_ref, v_ref, lam_ref, subln_ref, o_ref, vext_ref, *, tq, lam_init):
    i = pl.program_id(2)

    @pl.when(i == 0)
    def _():
        v = v_ref[0]
        vext_ref[:, 0:LANES] = v
        vext_ref[:, LANES:2 * LANES] = jnp.ones_like(v)

    qs = _split_heads(q_ref[0])
    row = lax.broadcasted_iota(jnp.int32, (tq, tq), 0)
    col = lax.broadcasted_iota(jnp.int32, (tq, tq), 1)
    visible = (col // CHUNK) <= (row // CHUNK)

    def block(kb, carry, diag):
        start = pl.multiple_of(kb * tq, tq)
        k = k_ref[0, pl.ds(start, tq), :]
        v = vext_ref[pl.ds(start, tq), :]
        out = []
        for h in range(2):
            s = _dot_nt(qs[h], k)
            if diag:
                s = jnp.where(visible, s, -jnp.inf)
            out.append(_softmax_step(s, carry[h], v))
        return tuple(out)

    init = (jnp.full((tq, 1), -jnp.inf, F32), jnp.zeros((tq, 2 * LANES), F32))
    carry = block(i, (init, init), True)
    carry = lax.fori_loop(0, i, lambda n, c: block(i - 1 - n, c, False), carry)

    lf = lam_ref[...]
    lam = (jnp.exp(jnp.sum(lf[0:1] * lf[1:2], axis=1, keepdims=True))
           - jnp.exp(jnp.sum(lf[2:3] * lf[3:4], axis=1, keepdims=True)) + lam_init)
    a1, a2 = (acc[:, :LANES] / acc[:, LANES:] for _, acc in carry)
    o = a1 - lam * a2
    o_ref[0] = (_rms(o, subln_ref[...]) * (1.0 - lam_init)).astype(o_ref.dtype)


def _attn_diff(yc, diff_lambda, subln, tq, lam_init):
    b, s, _ = yc.shape
    return pl.pallas_call(
        functools.partial(_attn_diff_kernel, tq=tq, lam_init=lam_init),
        grid=(b, H_DIFF, s // tq),
        in_specs=[pl.BlockSpec((1, tq, LANES), lambda bi, j, i: (bi, i, j)),
                  pl.BlockSpec((1, s, LANES), lambda bi, j, i: (bi, 0, H_DIFF + j)),
                  pl.BlockSpec((1, s, LANES), lambda bi, j, i: (bi, 0, 2 * H_DIFF + j)),
                  _const_spec(diff_lambda.shape), _const_spec(subln.shape)],
        out_specs=pl.BlockSpec((1, tq, LANES), lambda bi, j, i: (bi, i, j)),
        out_shape=jax.ShapeDtypeStruct((b, s, W_DIFF), BF16),
        scratch_shapes=[pltpu.VMEM((s, 2 * LANES), BF16)],
        compiler_params=_params(3),
        name="attn_diff",
    )(yc, yc, yc, diff_lambda, subln)


def _pair_tile(x):
    return jnp.concatenate([x, x], axis=-1)


def kernel(x, p, positions, ffn1_norm, ffn1_wi, ffn1_wo, mix_norm, w_in, b_forget, qk_gain_fox, qk_gain_diff, diff_lambda, diff_subln, w_br, w_o, ffn2_norm, ffn2_wi, ffn2_wo, ple_norm, ple_gate_w, ple_proj_w):
    b, s, d = x.shape
    depth = p.shape[0]
    t = b * s
    tm = min(512, t)
    tq = min(128, s)
    n_chunks = D_FF // FF_CHUNK

    idx = jnp.arange(LANES)
    grp = jnp.where((idx[:, None] // HEAD_DIM) == (idx[None, :] // HEAD_DIM), 1.0 / HEAD_DIM, 0.0).astype(BF16)
    tri_incl = (idx[:, None] <= idx[None, :]).astype(BF16)
    kq = jnp.arange(tq)
    uo = jnp.concatenate([(kq[:, None] > kq[None, :]).astype(BF16), jnp.ones((tq, tq), BF16)], axis=1)

    rc, rs1, rs2 = _rope_tables(positions, tm)

    def ffn_weights(wi, wo):
        wa = wi[:, :D_FF].reshape(d, n_chunks, FF_CHUNK).transpose(1, 0, 2).astype(BF16)
        wg = wi[:, D_FF:].reshape(d, n_chunks, FF_CHUNK).transpose(1, 0, 2).astype(BF16)
        return wa, wg, wo.reshape(n_chunks, FF_CHUNK, d).astype(BF16)

    fb_cols = jnp.array([0, 1, 8, 9])
    h = x.reshape(t, d)
    for i in range(depth):
        lam_init = 0.8 - 0.6 * math.exp(-0.3 * i)
        o = IN_OFFSETS
        fb_w = jnp.zeros((d, LANES), F32).at[:, fb_cols].set(w_in[i][:, o["fb"]:o["c"]])
        w_proj = jnp.concatenate([w_in[i][:, o["a"]:o["fb"]], w_in[i][:, o["c"]:o["gates"]], fb_w], axis=1).astype(BF16)
        w_gate = w_in[i][:, o["gates"]:o["end"]].astype(BF16)
        fb_bias = jnp.zeros((1, LANES), F32).at[0, fb_cols].set(b_forget[i])

        h = _ffn(h, ffn1_norm[i][None], *ffn_weights(ffn1_wi[i], ffn1_wo[i]), tm)
        ya, yb, yc, fb = _inproj(h, mix_norm[i][None], w_proj, _pair_tile(qk_gain_fox[i]),
                                 _pair_tile(qk_gain_diff[i]), grp, rc, rs1, rs2, tm)
        negcum = _forget_cumsum(fb.reshape(b, s, LANES), fb_bias, tri_incl)
        oa = _attn_sb(ya.reshape(b, s, -1), uo, tq)
        ob = _attn_fox(yb.reshape(b, s, -1), negcum, tq)
        oc = _attn_diff(yc.reshape(b, s, -1), diff_lambda[i], diff_subln[i][None], tq, lam_init)
        h = _merge(h, oa.reshape(t, -1), ob.reshape(t, -1), oc.reshape(t, -1), mix_norm[i][None],
                   w_gate, w_br[i].astype(BF16), w_o[i].astype(BF16), tm)
        h = _ffn(h, ffn2_norm[i][None], *ffn_weights(ffn2_wi[i], ffn2_wo[i]), tm)
        h = _ple(h, p[i].reshape(t, P_DIM), ple_norm[i][None], ple_gate_w[i].astype(BF16),
                 ple_proj_w[i].astype(BF16), tm)
    return h.reshape(b, s, d)
```

```python
import functools
import math

import jax
import jax.numpy as jnp
from jax import lax
from jax.experimental import pallas as pl
from jax.experimental.pallas import tpu as pltpu

D_MODEL = 1024
HEAD_DIM = 64
LANES = 128
N_PAIR_SB = 2
N_PAIR_FOX = 2
H_DIFF = 4
W_SB = 256
W_FOX = 256
W_DIFF = 512
ROT_DIM = 16
ROPE_THETA = 500000.0
D_FF = 2816
FF_CHUNK = 256
ATTN_BLOCK = 256
FOX_TILES = 2
DIFF_TILES = 2
P_DIM = 256
EPS = 1e-6
CHUNK = 64
LOG2E = math.log2(math.e)
SCALE = HEAD_DIM ** -0.5 * LOG2E
IN_OFFSETS = {"a": 0, "b": 768, "fb": 1536, "c": 1540, "gates": 3076, "end": 6148}
VMEM_LIMIT = 56 * 1024 * 1024

F32 = jnp.float32
BF16 = jnp.bfloat16
NT_DIMS = (((1,), (1,)), ((), ()))


def _dot(a, b):
    return jnp.dot(a, b, preferred_element_type=F32)


def _dot_nt(a, b):
    return lax.dot_general(a, b, NT_DIMS, preferred_element_type=F32)


def _rms(x, gain):
    ms = jnp.mean(x * x, axis=-1, keepdims=True)
    return x * lax.rsqrt(ms + EPS) * gain


def _split_bf16(x, parts):
    out = []
    for _ in range(parts - 1):
        piece = x.astype(BF16)
        out.append(piece)
        x = x - piece.astype(F32)
    out.append(x.astype(BF16))
    return out


def _params(n_grid):
    return pltpu.CompilerParams(
        dimension_semantics=("arbitrary",) * n_grid, vmem_limit_bytes=VMEM_LIMIT)


def _const_spec(shape):
    zeros = (0,) * len(shape)
    return pl.BlockSpec(shape, lambda *_: zeros, pipeline_mode=pl.Buffered(1))


def _ffn_kernel(h_ref, gain_ref, wa_ref, wg_ref, wo_ref, o_ref, acc_ref):
    h = h_ref[...]
    xn = _rms(h, gain_ref[...]).astype(BF16)
    acc_ref[...] = jnp.zeros_like(acc_ref)

    def chunk(c, carry):
        a = _dot(xn, wa_ref[c])
        g = _dot(xn, wg_ref[c])
        act = (a * jax.nn.sigmoid(a) * g).astype(BF16)
        acc_ref[...] += _dot(act, wo_ref[c])
        return carry

    lax.fori_loop(0, wa_ref.shape[0], chunk, 0)
    o_ref[...] = h + 0.5 * acc_ref[...]


def _ffn(h, gain, wa, wg, wo, tm):
    t = h.shape[0]
    row = pl.BlockSpec((tm, D_MODEL), lambda i: (i, 0))
    return pl.pallas_call(
        _ffn_kernel,
        grid=(t // tm,),
        in_specs=[row, _const_spec(gain.shape), _const_spec(wa.shape), _const_spec(wg.shape),
                  _const_spec(wo.shape)],
        out_specs=row,
        out_shape=jax.ShapeDtypeStruct(h.shape, F32),
        scratch_shapes=[pltpu.VMEM((tm, D_MODEL), F32)],
        compiler_params=_params(1),
        name="ffn",
    )(h, gain, wa, wg, wo)


def _rope_kernel(pos_ref, const_ref, c_ref, s1_ref, s2_ref):
    ang = pos_ref[...].astype(F32) * const_ref[0:1, :]
    sin = jnp.sin(ang)
    c_ref[...] = jnp.cos(ang)
    s1_ref[...] = sin * const_ref[1:2, :]
    s2_ref[...] = sin * const_ref[2:3, :]


def _rope_tables(positions, tm):
    t = positions.size
    lane = jnp.arange(LANES) % HEAD_DIM
    inv_freq = ROPE_THETA ** (-jnp.arange(0, ROT_DIM, 2, dtype=F32) / ROT_DIM)
    half = ROT_DIM // 2
    freq_lane = jnp.where(lane < ROT_DIM, inv_freq[lane % half], 0.0)
    plus = jnp.where((lane >= half) & (lane < ROT_DIM), 1.0, 0.0)
    minus = jnp.where(lane < half, -1.0, 0.0)
    const = jnp.zeros((8, LANES), F32).at[0].set(freq_lane).at[1].set(plus).at[2].set(minus)
    out = jax.ShapeDtypeStruct((t, LANES), F32)
    row = pl.BlockSpec((tm, LANES), lambda i: (i, 0))
    return pl.pallas_call(
        _rope_kernel,
        grid=(t // tm,),
        in_specs=[pl.BlockSpec((tm, 1), lambda i: (i, 0)), _const_spec(const.shape)],
        out_specs=[row, row, row],
        out_shape=[out, out, out],
        compiler_params=_params(1),
        name="rope_tables",
    )(positions.reshape(t, 1), const)


def _inproj_kernel(h_ref, gain_ref, w_ref, gfox_ref, gdiff_ref, grp_ref, rc_ref, rs1_ref, rs2_ref,
                   ya_ref, yb_ref, yc_ref, fb_ref):
    xn = _rms(h_ref[...], gain_ref[...]).astype(BF16)
    grp = grp_ref[...]

    def head_norm(x, gain):
        ms = _dot((x * x).astype(BF16), grp)
        return x * lax.rsqrt(ms + EPS) * gain

    def rope(x):
        return (x * rc_ref[...] + pltpu.roll(x, ROT_DIM // 2, 1) * rs1_ref[...]
                + pltpu.roll(x, LANES - ROT_DIM // 2, 1) * rs2_ref[...])

    y = _dot(xn, w_ref[:, 0:768])
    ya_ref[:, 0:W_SB] = (y[:, 0:W_SB] * SCALE).astype(BF16)
    ya_ref[:, W_SB:768] = y[:, W_SB:768].astype(BF16)

    y = _dot(xn, w_ref[:, 768:1536])
    for blk in range(2 * N_PAIR_FOX):
        is_q = blk < N_PAIR_FOX
        x = head_norm(y[:, blk * LANES:(blk + 1) * LANES], gfox_ref[(0 if is_q else 1):(1 if is_q else 2), :])
        if is_q:
            x = x * SCALE
        yb_ref[:, blk * LANES:(blk + 1) * LANES] = x.astype(BF16)
    yb_ref[:, 2 * W_FOX:768] = y[:, 2 * W_FOX:768].astype(BF16)

    y = _dot(xn, w_ref[:, 1536:3072])
    for blk in range(2 * H_DIFF):
        is_q = blk < H_DIFF
        x = head_norm(y[:, blk * LANES:(blk + 1) * LANES], gdiff_ref[(0 if is_q else 1):(1 if is_q else 2), :])
        x = rope(x)
        if is_q:
            x = x * SCALE
        yc_ref[:, blk * LANES:(blk + 1) * LANES] = x.astype(BF16)
    yc_ref[:, 2 * W_DIFF:1536] = y[:, 2 * W_DIFF:1536].astype(BF16)

    fb_ref[...] = _dot(xn, w_ref[:, 3072:3200])


def _inproj(h, gain, w, gfox, gdiff, grp, rc, rs1, rs2, tm):
    t = h.shape[0]
    row = lambda n: pl.BlockSpec((tm, n), lambda i: (i, 0))
    return pl.pallas_call(
        _inproj_kernel,
        grid=(t // tm,),
        in_specs=[row(D_MODEL), _const_spec(gain.shape), _const_spec(w.shape), _const_spec(gfox.shape),
                  _const_spec(gdiff.shape), _const_spec(grp.shape), row(LANES), row(LANES), row(LANES)],
        out_specs=[row(768), row(768), row(1536), row(LANES)],
        out_shape=[jax.ShapeDtypeStruct((t, 768), BF16), jax.ShapeDtypeStruct((t, 768), BF16),
                   jax.ShapeDtypeStruct((t, 1536), BF16), jax.ShapeDtypeStruct((t, LANES), F32)],
        compiler_params=_params(1),
        name="inproj",
    )(h, gain, w, gfox, gdiff, grp, rc, rs1, rs2)


def _forget_kernel(fb_ref, bias_ref, tri_ref, o_ref):
    s = fb_ref.shape[1]
    tri = tri_ref[...]
    carry = jnp.zeros((16, 1), F32)
    for blk in range(s // LANES):
        x = fb_ref[0, blk * LANES:(blk + 1) * LANES, :] + bias_ref[...]
        log_f = jnp.minimum(x, 0.0) - jnp.log(1.0 + jnp.exp(-jnp.abs(x)))
        lt = log_f.T[0:16, :]
        cum = carry
        for piece in _split_bf16(lt, 3):
            cum = cum + _dot(piece, tri)
        o_ref[0, 0, :, blk * LANES:(blk + 1) * LANES] = cum[0:8, :] * -LOG2E
        o_ref[0, 1, :, blk * LANES:(blk + 1) * LANES] = cum[8:16, :] * -LOG2E
        carry = cum[:, LANES - 1:LANES]


def _forget_cumsum(fb, bias, tri):
    b, s, _ = fb.shape
    return pl.pallas_call(
        _forget_kernel,
        grid=(b,),
        in_specs=[pl.BlockSpec((1, s, LANES), lambda i: (i, 0, 0)), _const_spec(bias.shape),
                  _const_spec(tri.shape)],
        out_specs=pl.BlockSpec((1, 2, 8, s), lambda i: (i, 0, 0, 0)),
        out_shape=jax.ShapeDtypeStruct((b, 2, 8, s), F32),
        compiler_params=_params(1),
        name="forget_cumsum",
    )(fb, bias, tri)


def _merge_kernel(h_ref, oa_ref, ob_ref, oc_ref, gain_ref, wgate_ref, wbr_ref, wo_ref, o_ref):
    h = h_ref[...]
    u = _rms(h, gain_ref[...]).astype(BF16)
    merged = None
    row0 = 0
    for br, o_br in enumerate((oa_ref, ob_ref, oc_ref)):
        width = o_br.shape[1]
        y = _dot(o_br[...], wbr_ref[row0:row0 + width, :])
        gate = jax.nn.sigmoid(_dot(u, wgate_ref[:, br * D_MODEL:(br + 1) * D_MODEL]))
        merged = gate * y if merged is None else merged + gate * y
        row0 += width
    o_ref[...] = h + _dot(merged.astype(BF16), wo_ref[...])


def _merge(h, oa, ob, oc, gain, wgate, wbr, wo, tm):
    t = h.shape[0]
    row = lambda n: pl.BlockSpec((tm, n), lambda i: (i, 0))
    return pl.pallas_call(
        _merge_kernel,
        grid=(t // tm,),
        in_specs=[row(D_MODEL), row(W_SB), row(W_FOX), row(W_DIFF), _const_spec(gain.shape),
                  _const_spec(wgate.shape), _const_spec(wbr.shape), _const_spec(wo.shape)],
        out_specs=row(D_MODEL),
        out_shape=jax.ShapeDtypeStruct(h.shape, F32),
        compiler_params=_params(1),
        name="merge",
    )(h, oa, ob, oc, gain, wgate, wbr, wo)


def _ple_kernel(h_ref, p_ref, gain_ref, wgate_ref, wproj_ref, o_ref):
    h = h_ref[...]
    u = _rms(h, gain_ref[...]).astype(BF16)
    gate = jax.nn.sigmoid(_dot(u, wgate_ref[...]))
    o_ref[...] = h + gate * _dot(p_ref[...].astype(BF16), wproj_ref[...])


def _ple(h, p, gain, wgate, wproj, tm):
    t = h.shape[0]
    row = lambda n: pl.BlockSpec((tm, n), lambda i: (i, 0))
    return pl.pallas_call(
        _ple_kernel,
        grid=(t // tm,),
        in_specs=[row(D_MODEL), row(P_DIM), _const_spec(gain.shape), _const_spec(wgate.shape),
                  _const_spec(wproj.shape)],
        out_specs=row(D_MODEL),
        out_shape=jax.ShapeDtypeStruct(h.shape, F32),
        compiler_params=_params(1),
        name="ple",
    )(h, p, gain, wgate, wproj)


def _low_lanes():
    return lax.broadcasted_iota(jnp.int32, (1, LANES), 1) < HEAD_DIM


def _block_iota(tq, axis):
    return lax.broadcasted_iota(jnp.int32, (tq, tq), axis)


def _block_ds(kb, tq):
    if isinstance(kb, int):
        return pl.ds(kb * tq, tq)
    return pl.ds(pl.multiple_of(kb * tq, tq), tq)


def _split_heads(q):
    low = _low_lanes()
    zero = jnp.zeros_like(q)
    return jnp.where(low, q, zero), jnp.where(low, zero, q)


def _attn_sb_kernel(q_ref, k_ref, v_ref, uo_ref, o_ref, lb_ref, lk_ref, a_ref, acc_ref, *, tq):
    i = pl.program_id(2)
    qs = _split_heads(q_ref[0])
    strict = _block_iota(tq, 1) < _block_iota(tq, 0)
    uo = uo_ref[...]
    heads = range(2)
    n_tiles = tq // LANES

    def logits_stage(kb, slot, diag):
        k = k_ref[0, _block_ds(kb, tq), :]
        for h in heads:
            z = _dot_nt(qs[h], k)
            log_beta = jnp.minimum(z, 0.0) - jnp.log2(1.0 + jnp.exp2(-jnp.abs(z)))
            log_keep = log_beta - z
            if diag:
                log_keep = jnp.where(strict, log_keep, 0.0)
                log_beta = jnp.where(strict, log_beta, -jnp.inf)
            lb_ref[slot, h] = log_beta
            hi, lo = _split_bf16(log_keep, 2)
            for c in range(n_tiles):
                lk_ref[slot, h, :, 2 * c * LANES:(2 * c + 1) * LANES] = hi[:, c * LANES:(c + 1) * LANES]
                lk_ref[slot, h, :, (2 * c + 1) * LANES:(2 * c + 2) * LANES] = lo[:, c * LANES:(c + 1) * LANES]

    def weights_stage(slot, h, run):
        for c in reversed(range(n_tiles)):
            sums = _dot(lk_ref[slot, h, :, 2 * c * LANES:(2 * c + 2) * LANES], uo)
            sl = slice(c * LANES, (c + 1) * LANES)
            a_ref[slot, h, :, sl] = jnp.exp2(lb_ref[slot, h, :, sl] + sums[:, :LANES] + run).astype(BF16)
            run = run + sums[:, LANES:]
        return run

    def accumulate(slot, kb):
        v = v_ref[0, _block_ds(kb, tq), :]
        for h in heads:
            acc_ref[h] += _dot(a_ref[slot, h], v)

    def step(n, runs):
        cur = n % 2
        kb = i - n
        accumulate(1 - cur, jnp.minimum(kb + 1, i))
        runs = tuple(weights_stage(cur, h, runs[h]) for h in heads)
        logits_stage(jnp.maximum(kb - 1, 0), 1 - cur, False)
        return runs

    logits_stage(i, 0, True)
    a_ref[1] = jnp.zeros(a_ref.shape[1:], BF16)
    acc_ref[...] = jnp.zeros_like(acc_ref)
    zero_row = jnp.zeros((tq, LANES), F32)
    lax.fori_loop(0, i + 1, step, (zero_row, zero_row))
    accumulate(i % 2, 0)
    o_ref[0] = jnp.where(_low_lanes(), acc_ref[0], acc_ref[1]).astype(o_ref.dtype)


def _attn_sb(ya, uo, tq):
    b, s, _ = ya.shape
    return pl.pallas_call(
        functools.partial(_attn_sb_kernel, tq=tq),
        grid=(b, N_PAIR_SB, s // tq),
        in_specs=[pl.BlockSpec((1, tq, LANES), lambda bi, j, i: (bi, i, j)),
                  pl.BlockSpec((1, s, LANES), lambda bi, j, i: (bi, 0, N_PAIR_SB + j)),
                  pl.BlockSpec((1, s, LANES), lambda bi, j, i: (bi, 0, 2 * N_PAIR_SB + j)),
                  _const_spec(uo.shape)],
        out_specs=pl.BlockSpec((1, tq, LANES), lambda bi, j, i: (bi, i, j)),
        out_shape=jax.ShapeDtypeStruct((b, s, W_SB), BF16),
        scratch_shapes=[pltpu.VMEM((2, 2, tq, tq), F32), pltpu.VMEM((2, 2, tq, 2 * tq), BF16),
                        pltpu.VMEM((2, 2, tq, tq), BF16), pltpu.VMEM((2, tq, LANES), F32)],
        compiler_params=_params(3),
        name="attn_stickbreak",
    )(ya, ya, ya, uo)


SOFTMAX_ROWS = 64


def _softmax_scratch(n_streams, tq, acc_width):
    return [pltpu.VMEM((2, n_streams, tq, tq), F32), pltpu.VMEM((2, n_streams, tq, tq), BF16),
            pltpu.VMEM((n_streams, tq, acc_width), F32)]


def _softmax_walk(i, diag_scores, scores, values, s_ref, p_ref, acc_ref, tq):
    n_st = len(diag_scores)
    streams = range(n_st)
    chunks = range(tq // SOFTMAX_ROWS)

    def rows(c):
        return slice(c * SOFTMAX_ROWS, (c + 1) * SOFTMAX_ROWS)

    def accumulate(slot, alpha, kb):
        v = values(kb)
        for st in streams:
            acc_ref[st] = acc_ref[st] * alpha[st] + _dot(p_ref[slot, st], v[st])

    def step(n, carry):
        m, alpha_prev = carry
        cur = n % 2
        kb = i - n
        accumulate(1 - cur, alpha_prev, jnp.minimum(kb + 1, i))
        m_new, alpha = [], []
        for st in streams:
            m_st, alpha_st = [], []
            for c in chunks:
                s = s_ref[cur, st, rows(c), :]
                m_old = m[st][rows(c)]
                m_c = jnp.maximum(m_old, jnp.max(s, axis=1, keepdims=True))
                p_ref[cur, st, rows(c), :] = jnp.exp2(s - m_c).astype(BF16)
                m_st.append(m_c)
                alpha_st.append(jnp.exp2(m_old - m_c))
            m_new.append(jnp.concatenate(m_st, axis=0))
            alpha.append(jnp.concatenate(alpha_st, axis=0))
        s_next = scores(jnp.maximum(kb - 1, 0))
        for st in streams:
            s_ref[1 - cur, st] = s_next[st]
        return tuple(m_new), tuple(alpha)

    for st in streams:
        s_ref[0, st] = diag_scores[st]
    p_ref[1] = jnp.zeros(p_ref.shape[1:], BF16)
    acc_ref[...] = jnp.zeros_like(acc_ref)
    init = ((jnp.full((tq, 1), -jnp.inf, F32),) * n_st, (jnp.ones((tq, 1), F32),) * n_st)
    _, alpha_last = lax.fori_loop(0, i + 1, step, init)
    accumulate(i % 2, alpha_last, 0)
    return [acc_ref[st] for st in streams]


def _attn_fox_kernel(q_ref, k_ref, v_ref, nc_ref, o_ref, vones_ref, s_ref, p_ref, acc_ref, *, tq):
    i = pl.program_id(2)
    low = _low_lanes()
    tiles = q_ref.shape[2] // LANES

    def tile(c):
        return slice(c * LANES, (c + 1) * LANES)

    @pl.when(i == 0)
    def _():
        for c in range(tiles):
            v = v_ref[0, :, tile(c)]
            one = jnp.ones_like(v)
            vones_ref[2 * c] = jnp.where(low, v, one)
            vones_ref[2 * c + 1] = jnp.where(low, one, v)

    qs = [qh for c in range(tiles) for qh in _split_heads(q_ref[0, :, tile(c)])]

    def scores(kb):
        out = []
        for c in range(tiles):
            k = k_ref[0, _block_ds(kb, tq), tile(c)]
            for h in range(2):
                out.append(_dot_nt(qs[2 * c + h], k) + nc_ref[0, c, h:h + 1, _block_ds(kb, tq)])
        return tuple(out)

    def values(kb):
        return tuple(vones_ref[st, _block_ds(kb, tq), :] for st in range(2 * tiles))

    causal = _block_iota(tq, 1) <= _block_iota(tq, 0)
    diag = tuple(jnp.where(causal, s, -jnp.inf) for s in scores(i))
    acc = _softmax_walk(i, diag, scores, values, s_ref, p_ref, acc_ref, tq)
    outs = [a / pltpu.roll(a, HEAD_DIM, 1) for a in acc]
    for c in range(tiles):
        o_ref[0, :, tile(c)] = jnp.where(low, outs[2 * c], outs[2 * c + 1]).astype(o_ref.dtype)


def _attn_fox(yb, negcum, tq, tiles):
    b, s, _ = yb.shape
    groups = N_PAIR_FOX // tiles
    width = tiles * LANES
    return pl.pallas_call(
        functools.partial(_attn_fox_kernel, tq=tq),
        grid=(b, groups, s // tq),
        in_specs=[pl.BlockSpec((1, tq, width), lambda bi, j, i: (bi, i, j)),
                  pl.BlockSpec((1, s, width), lambda bi, j, i: (bi, 0, groups + j)),
                  pl.BlockSpec((1, s, width), lambda bi, j, i: (bi, 0, 2 * groups + j)),
                  pl.BlockSpec((1, tiles, 8, s), lambda bi, j, i: (bi, j, 0, 0))],
        out_specs=pl.BlockSpec((1, tq, width), lambda bi, j, i: (bi, i, j)),
        out_shape=jax.ShapeDtypeStruct((b, s, W_FOX), BF16),
        scratch_shapes=[pltpu.VMEM((2 * tiles, s, LANES), BF16)] + _softmax_scratch(2 * tiles, tq, LANES),
        compiler_params=_params(3),
        name="attn_forget",
    )(yb, yb, yb, negcum)


def _attn_diff_kernel(q_ref, k_ref, v_ref, lam_ref, subln_ref, o_ref, vext_ref, s_ref, p_ref, acc_ref, *, tq,
                      lam_init):
    i = pl.program_id(2)
    tiles = q_ref.shape[2] // LANES

    def tile(c):
        return slice(c * LANES, (c + 1) * LANES)

    @pl.when(i == 0)
    def _():
        for c in range(tiles):
            v = v_ref[0, :, tile(c)]
            vext_ref[c, :, 0:LANES] = v
            vext_ref[c, :, LANES:2 * LANES] = jnp.ones_like(v)

    qs = [qh for c in range(tiles) for qh in _split_heads(q_ref[0, :, tile(c)])]

    def scores(kb):
        out = []
        for c in range(tiles):
            k = k_ref[0, _block_ds(kb, tq), tile(c)]
            out += [_dot_nt(qs[2 * c + h], k) for h in range(2)]
        return tuple(out)

    def values(kb):
        out = []
        for c in range(tiles):
            v = vext_ref[c, _block_ds(kb, tq), :]
            out += [v, v]
        return tuple(out)

    visible = (_block_iota(tq, 1) // CHUNK) <= (_block_iota(tq, 0) // CHUNK)
    diag = tuple(jnp.where(visible, s, -jnp.inf) for s in scores(i))
    acc = _softmax_walk(i, diag, scores, values, s_ref, p_ref, acc_ref, tq)

    lf = lam_ref[...]
    lam = (jnp.exp(jnp.sum(lf[0:1] * lf[1:2], axis=1, keepdims=True))
           - jnp.exp(jnp.sum(lf[2:3] * lf[3:4], axis=1, keepdims=True)) + lam_init)
    for c in range(tiles):
        a1, a2 = (a[:, :LANES] / a[:, LANES:] for a in acc[2 * c:2 * c + 2])
        o = a1 - lam * a2
        o_ref[0, :, tile(c)] = (_rms(o, subln_ref[...]) * (1.0 - lam_init)).astype(o_ref.dtype)


def _attn_diff(yc, diff_lambda, subln, tq, tiles, lam_init):
    b, s, _ = yc.shape
    groups = H_DIFF // tiles
    width = tiles * LANES
    return pl.pallas_call(
        functools.partial(_attn_diff_kernel, tq=tq, lam_init=lam_init),
        grid=(b, groups, s // tq),
        in_specs=[pl.BlockSpec((1, tq, width), lambda bi, j, i: (bi, i, j)),
                  pl.BlockSpec((1, s, width), lambda bi, j, i: (bi, 0, groups + j)),
                  pl.BlockSpec((1, s, width), lambda bi, j, i: (bi, 0, 2 * groups + j)),
                  _const_spec(diff_lambda.shape), _const_spec(subln.shape)],
        out_specs=pl.BlockSpec((1, tq, width), lambda bi, j, i: (bi, i, j)),
        out_shape=jax.ShapeDtypeStruct((b, s, W_DIFF), BF16),
        scratch_shapes=[pltpu.VMEM((tiles, s, 2 * LANES), BF16)] + _softmax_scratch(2 * tiles, tq, 2 * LANES),
        compiler_params=_params(3),
        name="attn_diff",
    )(yc, yc, yc, diff_lambda, subln)


def _pair_tile(x):
    return jnp.concatenate([x, x], axis=-1)


def kernel(x, p, positions, ffn1_norm, ffn1_wi, ffn1_wo, mix_norm, w_in, b_forget, qk_gain_fox, qk_gain_diff, diff_lambda, diff_subln, w_br, w_o, ffn2_norm, ffn2_wi, ffn2_wo, ple_norm, ple_gate_w, ple_proj_w):
    b, s, d = x.shape
    depth = p.shape[0]
    t = b * s
    tm = min(512, t)
    tq = min(ATTN_BLOCK, s)
    n_chunks = D_FF // FF_CHUNK

    idx = jnp.arange(LANES)
    grp = jnp.where((idx[:, None] // HEAD_DIM) == (idx[None, :] // HEAD_DIM), 1.0 / HEAD_DIM, 0.0).astype(BF16)
    tri_incl = (idx[:, None] <= idx[None, :]).astype(BF16)
    uo = jnp.concatenate([(idx[:, None] > idx[None, :]).astype(BF16), jnp.ones((LANES, LANES), BF16)], axis=1)
    uo = jnp.concatenate([uo, uo], axis=0)

    rc, rs1, rs2 = _rope_tables(positions, tm)

    def ffn_weights(wi, wo):
        wa = wi[:, :D_FF].reshape(d, n_chunks, FF_CHUNK).transpose(1, 0, 2).astype(BF16)
        wg = wi[:, D_FF:].reshape(d, n_chunks, FF_CHUNK).transpose(1, 0, 2).astype(BF16)
        return wa, wg, wo.reshape(n_chunks, FF_CHUNK, d).astype(BF16)

    fb_cols = jnp.array([0, 1, 8, 9])
    h = x.reshape(t, d)
    for i in range(depth):
        lam_init = 0.8 - 0.6 * math.exp(-0.3 * i)
        o = IN_OFFSETS
        fb_w = jnp.zeros((d, LANES), F32).at[:, fb_cols].set(w_in[i][:, o["fb"]:o["c"]])
        w_proj = jnp.concatenate([w_in[i][:, o["a"]:o["fb"]], w_in[i][:, o["c"]:o["gates"]], fb_w], axis=1).astype(BF16)
        w_gate = w_in[i][:, o["gates"]:o["end"]].astype(BF16)
        fb_bias = jnp.zeros((1, LANES), F32).at[0, fb_cols].set(b_forget[i])

        h = _ffn(h, ffn1_norm[i][None], *ffn_weights(ffn1_wi[i], ffn1_wo[i]), tm)
        ya, yb, yc, fb = _inproj(h, mix_norm[i][None], w_proj, _pair_tile(qk_gain_fox[i]),
                                 _pair_tile(qk_gain_diff[i]), grp, rc, rs1, rs2, tm)
        negcum = _forget_cumsum(fb.reshape(b, s, LANES), fb_bias, tri_incl)
        oa = _attn_sb(ya.reshape(b, s, -1), uo, tq)
        ob = _attn_fox(yb.reshape(b, s, -1), negcum, tq, FOX_TILES)
        oc = _attn_diff(yc.reshape(b, s, -1), diff_lambda[i], diff_subln[i][None], tq, DIFF_TILES, lam_init)
        h = _merge(h, oa.reshape(t, -1), ob.reshape(t, -1), oc.reshape(t, -1), mix_norm[i][None],
                   w_gate, w_br[i].astype(BF16), w_o[i].astype(BF16), tm)
        h = _ffn(h, ffn2_norm[i][None], *ffn_weights(ffn2_wi[i], ffn2_wo[i]), tm)
        h = _ple(h, p[i].reshape(t, P_DIM), ple_norm[i][None], ple_gate_w[i].astype(BF16),
                 ple_proj_w[i].astype(BF16), tm)
    return h.reshape(b, s, d)
```

```python
import functools
import math

import jax
import jax.numpy as jnp
from jax import lax
from jax.experimental import pallas as pl
from jax.experimental.pallas import tpu as pltpu

D_MODEL = 1024
HEAD_DIM = 64
LANES = 128
N_PAIR_SB = 2
N_PAIR_FOX = 2
H_DIFF = 4
W_SB = 256
W_FOX = 256
W_DIFF = 512
ROT_DIM = 16
ROPE_THETA = 500000.0
D_FF = 2816
FF_CHUNK = 256
ATTN_BLOCK = 256
SB_PAIRS = 2
FOX_TILES = 2
DIFF_TILES = 4
P_DIM = 256
EPS = 1e-6
CHUNK = 64
LOG2E = math.log2(math.e)
SCALE = HEAD_DIM ** -0.5 * LOG2E
FB_COLS = N_PAIR_FOX * LANES
BIAS_PIECES = 3
IN_OFFSETS = {"a": 0, "b": 768, "fb": 1536, "c": 1540, "gates": 3076, "end": 6148}
VMEM_LIMIT = 56 * 1024 * 1024

F32 = jnp.float32
BF16 = jnp.bfloat16
NT_DIMS = (((1,), (1,)), ((), ()))


def _dot(a, b):
    return jnp.dot(a, b, preferred_element_type=F32)


def _dot_nt(a, b):
    return lax.dot_general(a, b, NT_DIMS, preferred_element_type=F32)


def _rms(x, gain):
    ms = jnp.mean(x * x, axis=-1, keepdims=True)
    return x * lax.rsqrt(ms + EPS) * gain


def _split_bf16(x, parts):
    out = []
    for _ in range(parts - 1):
        piece = x.astype(BF16)
        out.append(piece)
        x = x - piece.astype(F32)
    out.append(x.astype(BF16))
    return out


def _params(n_grid):
    return pltpu.CompilerParams(
        dimension_semantics=("arbitrary",) * n_grid, vmem_limit_bytes=VMEM_LIMIT)


def _const_spec(shape):
    zeros = (0,) * len(shape)
    return pl.BlockSpec(shape, lambda *_: zeros, pipeline_mode=pl.Buffered(1))


def _ffn_kernel(h_ref, gain_ref, wa_ref, wg_ref, wo_ref, o_ref, acc_ref):
    h = h_ref[...]
    xn = _rms(h, gain_ref[...]).astype(BF16)
    acc_ref[...] = jnp.zeros_like(acc_ref)

    def chunk(c, carry):
        a = _dot(xn, wa_ref[c])
        g = _dot(xn, wg_ref[c])
        act = (a * jax.nn.sigmoid(a) * g).astype(BF16)
        acc_ref[...] += _dot(act, wo_ref[c])
        return carry

    lax.fori_loop(0, wa_ref.shape[0], chunk, 0)
    o_ref[...] = h + 0.5 * acc_ref[...]


def _ffn(h, gain, wa, wg, wo, tm):
    t = h.shape[0]
    row = pl.BlockSpec((tm, D_MODEL), lambda i: (i, 0))
    return pl.pallas_call(
        _ffn_kernel,
        grid=(t // tm,),
        in_specs=[row, _const_spec(gain.shape), _const_spec(wa.shape), _const_spec(wg.shape),
                  _const_spec(wo.shape)],
        out_specs=row,
        out_shape=jax.ShapeDtypeStruct(h.shape, F32),
        scratch_shapes=[pltpu.VMEM((tm, D_MODEL), F32)],
        compiler_params=_params(1),
        name="ffn",
    )(h, gain, wa, wg, wo)


def _rope_kernel(pos_ref, const_ref, c_ref, s1_ref, s2_ref):
    ang = pos_ref[...].astype(F32) * const_ref[0:1, :]
    sin = jnp.sin(ang)
    c_ref[...] = jnp.cos(ang)
    s1_ref[...] = sin * const_ref[1:2, :]
    s2_ref[...] = sin * const_ref[2:3, :]


def _rope_tables(positions, tm):
    t = positions.size
    lane = jnp.arange(LANES) % HEAD_DIM
    inv_freq = ROPE_THETA ** (-jnp.arange(0, ROT_DIM, 2, dtype=F32) / ROT_DIM)
    half = ROT_DIM // 2
    freq_lane = jnp.where(lane < ROT_DIM, inv_freq[lane % half], 0.0)
    plus = jnp.where((lane >= half) & (lane < ROT_DIM), 1.0, 0.0)
    minus = jnp.where(lane < half, -1.0, 0.0)
    const = jnp.zeros((8, LANES), F32).at[0].set(freq_lane).at[1].set(plus).at[2].set(minus)
    out = jax.ShapeDtypeStruct((t, LANES), F32)
    row = pl.BlockSpec((tm, LANES), lambda i: (i, 0))
    return pl.pallas_call(
        _rope_kernel,
        grid=(t // tm,),
        in_specs=[pl.BlockSpec((tm, 1), lambda i: (i, 0)), _const_spec(const.shape)],
        out_specs=[row, row, row],
        out_shape=[out, out, out],
        compiler_params=_params(1),
        name="rope_tables",
    )(positions.reshape(t, 1), const)


def _inproj_kernel(h_ref, gain_ref, w_ref, gfox_ref, gdiff_ref, grp_ref, rc_ref, rs1_ref, rs2_ref,
                   ya_ref, yb_ref, yc_ref, fb_ref):
    xn = _rms(h_ref[...], gain_ref[...]).astype(BF16)
    grp = grp_ref[...]

    def head_norm(x, gain):
        ms = _dot((x * x).astype(BF16), grp)
        return x * lax.rsqrt(ms + EPS) * gain

    def rope(x):
        return (x * rc_ref[...] + pltpu.roll(x, ROT_DIM // 2, 1) * rs1_ref[...]
                + pltpu.roll(x, LANES - ROT_DIM // 2, 1) * rs2_ref[...])

    y = _dot(xn, w_ref[:, 0:768])
    ya_ref[:, 0:W_SB] = (y[:, 0:W_SB] * SCALE).astype(BF16)
    ya_ref[:, W_SB:768] = y[:, W_SB:768].astype(BF16)

    y = _dot(xn, w_ref[:, 768:1536])
    for blk in range(2 * N_PAIR_FOX):
        is_q = blk < N_PAIR_FOX
        x = head_norm(y[:, blk * LANES:(blk + 1) * LANES], gfox_ref[(0 if is_q else 1):(1 if is_q else 2), :])
        if is_q:
            x = x * SCALE
        yb_ref[:, blk * LANES:(blk + 1) * LANES] = x.astype(BF16)
    yb_ref[:, 2 * W_FOX:768] = y[:, 2 * W_FOX:768].astype(BF16)

    y = _dot(xn, w_ref[:, 1536:3072])
    for blk in range(2 * H_DIFF):
        is_q = blk < H_DIFF
        x = head_norm(y[:, blk * LANES:(blk + 1) * LANES], gdiff_ref[(0 if is_q else 1):(1 if is_q else 2), :])
        x = rope(x)
        if is_q:
            x = x * SCALE
        yc_ref[:, blk * LANES:(blk + 1) * LANES] = x.astype(BF16)
    yc_ref[:, 2 * W_DIFF:1536] = y[:, 2 * W_DIFF:1536].astype(BF16)

    fb_ref[...] = _dot(xn, w_ref[:, 3072:3072 + FB_COLS])


def _inproj(h, gain, w, gfox, gdiff, grp, rc, rs1, rs2, tm):
    t = h.shape[0]
    row = lambda n: pl.BlockSpec((tm, n), lambda i: (i, 0))
    return pl.pallas_call(
        _inproj_kernel,
        grid=(t // tm,),
        in_specs=[row(D_MODEL), _const_spec(gain.shape), _const_spec(w.shape), _const_spec(gfox.shape),
                  _const_spec(gdiff.shape), _const_spec(grp.shape), row(LANES), row(LANES), row(LANES)],
        out_specs=[row(768), row(768), row(1536), row(FB_COLS)],
        out_shape=[jax.ShapeDtypeStruct((t, 768), BF16), jax.ShapeDtypeStruct((t, 768), BF16),
                   jax.ShapeDtypeStruct((t, 1536), BF16), jax.ShapeDtypeStruct((t, FB_COLS), F32)],
        compiler_params=_params(1),
        name="inproj",
    )(h, gain, w, gfox, gdiff, grp, rc, rs1, rs2)


def _forget_kernel(fb_ref, bias_ref, tri_ref, o_ref):
    s = fb_ref.shape[1]
    tri = tri_ref[...]
    carry = jnp.zeros((1, fb_ref.shape[2]), F32)
    for blk in range(s // LANES):
        x = fb_ref[0, blk * LANES:(blk + 1) * LANES, :] + bias_ref[...]
        log_f = jnp.minimum(x, 0.0) - jnp.log(1.0 + jnp.exp(-jnp.abs(x)))
        cum = carry
        for piece in _split_bf16(log_f, 3):
            cum = cum + _dot(tri, piece)
        o_ref[0, blk * LANES:(blk + 1) * LANES, :] = cum * -LOG2E
        carry = cum[LANES - 1:LANES, :]


def _forget_cumsum(fb, bias, tri):
    b, s, cols = fb.shape
    spec = pl.BlockSpec((1, s, cols), lambda i: (i, 0, 0))
    return pl.pallas_call(
        _forget_kernel,
        grid=(b,),
        in_specs=[spec, _const_spec(bias.shape), _const_spec(tri.shape)],
        out_specs=spec,
        out_shape=jax.ShapeDtypeStruct(fb.shape, F32),
        compiler_params=_params(1),
        name="forget_cumsum",
    )(fb, bias, tri)


def _merge_kernel(h_ref, oa_ref, ob_ref, oc_ref, gain_ref, wgate_ref, wbr_ref, wo_ref, o_ref):
    h = h_ref[...]
    u = _rms(h, gain_ref[...]).astype(BF16)
    merged = None
    row0 = 0
    for br, o_br in enumerate((oa_ref, ob_ref, oc_ref)):
        width = o_br.shape[1]
        y = _dot(o_br[...], wbr_ref[row0:row0 + width, :])
        gate = jax.nn.sigmoid(_dot(u, wgate_ref[:, br * D_MODEL:(br + 1) * D_MODEL]))
        merged = gate * y if merged is None else merged + gate * y
        row0 += width
    o_ref[...] = h + _dot(merged.astype(BF16), wo_ref[...])


def _merge(h, oa, ob, oc, gain, wgate, wbr, wo, tm):
    t = h.shape[0]
    row = lambda n: pl.BlockSpec((tm, n), lambda i: (i, 0))
    return pl.pallas_call(
        _merge_kernel,
        grid=(t // tm,),
        in_specs=[row(D_MODEL), row(W_SB), row(W_FOX), row(W_DIFF), _const_spec(gain.shape),
                  _const_spec(wgate.shape), _const_spec(wbr.shape), _const_spec(wo.shape)],
        out_specs=row(D_MODEL),
        out_shape=jax.ShapeDtypeStruct(h.shape, F32),
        compiler_params=_params(1),
        name="merge",
    )(h, oa, ob, oc, gain, wgate, wbr, wo)


def _ple_kernel(h_ref, p_ref, gain_ref, wgate_ref, wproj_ref, o_ref):
    h = h_ref[...]
    u = _rms(h, gain_ref[...]).astype(BF16)
    gate = jax.nn.sigmoid(_dot(u, wgate_ref[...]))
    o_ref[...] = h + gate * _dot(p_ref[...].astype(BF16), wproj_ref[...])


def _ple(h, p, gain, wgate, wproj, tm):
    t = h.shape[0]
    row = lambda n: pl.BlockSpec((tm, n), lambda i: (i, 0))
    return pl.pallas_call(
        _ple_kernel,
        grid=(t // tm,),
        in_specs=[row(D_MODEL), row(P_DIM), _const_spec(gain.shape), _const_spec(wgate.shape),
                  _const_spec(wproj.shape)],
        out_specs=row(D_MODEL),
        out_shape=jax.ShapeDtypeStruct(h.shape, F32),
        compiler_params=_params(1),
        name="ple",
    )(h, p, gain, wgate, wproj)


TRANSPOSE_ROWS = 512
NO_BLOCK_LOGIT = -1e30


def _low_lanes():
    return lax.broadcasted_iota(jnp.int32, (1, LANES), 1) < HEAD_DIM


def _block_iota(tq, axis):
    return lax.broadcasted_iota(jnp.int32, (tq, tq), axis)


def _block_ds(kb, tq):
    if isinstance(kb, int):
        return pl.ds(kb * tq, tq)
    return pl.ds(pl.multiple_of(kb * tq, tq), tq)


def _mxu_transpose(x, eye_ref):
    n = x.shape[1]
    return _dot_nt(eye_ref[0:n, 0:n], x).astype(BF16)


def _for_each_transposed_chunk(src_ref, lanes, eye_ref, write):
    s = src_ref.shape[1]
    step = min(TRANSPOSE_ROWS, s)
    for start in range(0, s, step):
        pos = slice(start, start + step)
        write(pos, _mxu_transpose(src_ref[0, pos, lanes], eye_ref))


def _split_heads(q):
    low = _low_lanes()
    zero = jnp.zeros_like(q)
    return jnp.where(low, q, zero), jnp.where(low, zero, q)


def _attn_sb_kernel(q_ref, k_ref, v_ref, lu_ref, eye_ref, o_ref, vt_ref, lb_ref, lk_ref, a_ref, acc_ref, run_ref,
                    *, tq):
    i = pl.program_id(2)
    pairs = q_ref.shape[2] // LANES

    def tile(c):
        return slice(c * LANES, (c + 1) * LANES)

    @pl.when(i == 0)
    def _():
        for c in range(pairs):
            def write(pos, vt, c=c):
                vt_ref[tile(c), pos] = vt

            _for_each_transposed_chunk(v_ref, tile(c), eye_ref, write)

    qs = [qh for c in range(pairs) for qh in _split_heads(q_ref[0, :, tile(c)])]
    strict = _block_iota(tq, 0) < _block_iota(tq, 1)
    lu = lu_ref[...]
    heads = range(2 * pairs)
    n_tiles = tq // LANES

    def logits(kb):
        return [_dot_nt(k_ref[0, _block_ds(kb, tq), tile(h // 2)], qs[h]) for h in heads]

    def logits_stage(zs, slot, diag):
        for h in heads:
            z = zs[h]
            log_beta = jnp.minimum(z, 0.0) - jnp.log2(1.0 + jnp.exp2(-jnp.abs(z)))
            log_keep = log_beta - z
            if diag:
                log_keep = jnp.where(strict, log_keep, 0.0)
                log_beta = jnp.where(strict, log_beta, -jnp.inf)
            lb_ref[slot, h] = log_beta
            hi, lo = _split_bf16(log_keep, 2)
            for c in range(n_tiles):
                lk_ref[slot, h, 2 * c * LANES:(2 * c + 1) * LANES, :] = hi[c * LANES:(c + 1) * LANES, :]
                lk_ref[slot, h, (2 * c + 1) * LANES:(2 * c + 2) * LANES, :] = lo[c * LANES:(c + 1) * LANES, :]

    def suffix_sums(slot):
        return [[_dot(lu, lk_ref[slot, h, 2 * c * LANES:(2 * c + 2) * LANES, :]) for c in range(n_tiles)]
                for h in heads]

    def weights_stage(slot, h, sums, run):
        for c in reversed(range(n_tiles)):
            sl = slice(c * LANES, (c + 1) * LANES)
            a_ref[slot, h, sl, :] = jnp.exp2(lb_ref[slot, h, sl, :] + sums[c][0:LANES, :] + run).astype(BF16)
            run = run + sums[c][LANES:LANES + 1, :]
        return run

    def products(slot, kb):
        return [_dot(vt_ref[h * HEAD_DIM:(h + 1) * HEAD_DIM, _block_ds(kb, tq)], a_ref[slot, h]) for h in heads]

    def accumulate(pv):
        for h in heads:
            acc_ref[h] += pv[h]

    def step(n, cur):
        kb = i - n
        zs = logits(jnp.maximum(kb - 1, 0))
        if cur == 0:
            zs = [jnp.where(kb >= 1, z, NO_BLOCK_LOGIT) for z in zs]
        sums = suffix_sums(cur)
        pv = products(1 - cur, jnp.minimum(kb + 1, i))
        for h in heads:
            run_ref[h] = weights_stage(cur, h, sums[h], run_ref[h])
        logits_stage(zs, 1 - cur, False)
        accumulate(pv)

    def two_steps(t, carry):
        step(2 * t, 0)
        step(2 * t + 1, 1)
        return carry

    logits_stage(logits(i), 0, True)
    a_ref[1] = jnp.zeros(a_ref.shape[1:], BF16)
    acc_ref[...] = jnp.zeros_like(acc_ref)
    run_ref[...] = jnp.zeros_like(run_ref)
    lax.fori_loop(0, i // 2 + 1, two_steps, 0)
    accumulate(products(1, 0))
    for c in range(pairs):
        o_t = jnp.concatenate([acc_ref[2 * c], acc_ref[2 * c + 1]], axis=0).astype(BF16)
        o_ref[0, :, tile(c)] = _mxu_transpose(o_t, eye_ref)


def _attn_sb(ya, lu, eye, tq, pairs):
    b, s, _ = ya.shape
    groups = N_PAIR_SB // pairs
    width = pairs * LANES
    heads = 2 * pairs
    return pl.pallas_call(
        functools.partial(_attn_sb_kernel, tq=tq),
        grid=(b, groups, s // tq),
        in_specs=[pl.BlockSpec((1, tq, width), lambda bi, j, i: (bi, i, j)),
                  pl.BlockSpec((1, s, width), lambda bi, j, i: (bi, 0, groups + j)),
                  pl.BlockSpec((1, s, width), lambda bi, j, i: (bi, 0, 2 * groups + j)),
                  _const_spec(lu.shape), _const_spec(eye.shape)],
        out_specs=pl.BlockSpec((1, tq, width), lambda bi, j, i: (bi, i, j)),
        out_shape=jax.ShapeDtypeStruct((b, s, W_SB), BF16),
        scratch_shapes=[pltpu.VMEM((width, s), BF16),
                        pltpu.VMEM((2, heads, tq, tq), F32), pltpu.VMEM((2, heads, 2 * tq, tq), BF16),
                        pltpu.VMEM((2, heads, tq, tq), BF16), pltpu.VMEM((heads, HEAD_DIM, tq), F32),
                        pltpu.VMEM((heads, 1, tq), F32)],
        compiler_params=_params(3),
        name="attn_stickbreak",
    )(ya, ya, ya, lu, eye)


def _softmax_scratch(n_streams, tq, acc_rows):
    return [pltpu.VMEM((2, n_streams, tq, tq), F32), pltpu.VMEM((2, n_streams, tq, tq), BF16),
            pltpu.VMEM((n_streams, acc_rows, tq), F32), pltpu.VMEM((2, n_streams, 1, tq), F32)]


def _softmax_walk(i, diag_scores, scores, values, s_ref, p_ref, acc_ref, stat_ref, tq):
    n_st = len(diag_scores)
    streams = range(n_st)
    q_tiles = range(tq // LANES)

    def products(slot, kb):
        v = values(kb)
        return [_dot(v[st], p_ref[slot, st]) for st in streams]

    def accumulate(alpha, pv):
        for st in streams:
            acc_ref[st] = acc_ref[st] * alpha[st] + pv[st]

    def step(n, cur):
        kb = i - n
        s_next = scores(jnp.maximum(kb - 1, 0))
        if cur == 0:
            s_next = [jnp.where(kb >= 1, s, -jnp.inf) for s in s_next]
        pv = products(1 - cur, jnp.minimum(kb + 1, i))
        alpha_prev = [stat_ref[1, st] for st in streams]
        for st in streams:
            for j in q_tiles:
                cols = slice(j * LANES, (j + 1) * LANES)
                s = s_ref[cur, st, :, cols]
                m_old = stat_ref[0, st, :, cols]
                m_j = jnp.maximum(m_old, jnp.max(s, axis=0, keepdims=True))
                p_ref[cur, st, :, cols] = jnp.exp2(s - m_j).astype(BF16)
                stat_ref[0, st, :, cols] = m_j
                stat_ref[1, st, :, cols] = jnp.exp2(m_old - m_j)
        for st in streams:
            s_ref[1 - cur, st] = s_next[st]
        accumulate(alpha_prev, pv)

    def two_steps(t, carry):
        step(2 * t, 0)
        step(2 * t + 1, 1)
        return carry

    for st in streams:
        s_ref[0, st] = diag_scores[st]
    p_ref[1] = jnp.zeros(p_ref.shape[1:], BF16)
    acc_ref[...] = jnp.zeros_like(acc_ref)
    stat_ref[0] = jnp.full(stat_ref.shape[1:], -jnp.inf, F32)
    stat_ref[1] = jnp.ones(stat_ref.shape[1:], F32)
    lax.fori_loop(0, i // 2 + 1, two_steps, 0)
    accumulate([stat_ref[1, st] for st in streams], products(1, 0))
    return [acc_ref[st] for st in streams]


ONES_ROWS = 16


def _attn_fox_kernel(q_ref, k_ref, v_ref, nc_ref, eye_ref, o_ref, kx_ref, vt_ref, s_ref, p_ref, acc_ref,
                     stat_ref, *, tq):
    i = pl.program_id(2)
    low = _low_lanes()
    lane = lax.broadcasted_iota(jnp.int32, (1, LANES), 1)
    tiles = q_ref.shape[2] // LANES

    def tile(c):
        return slice(c * LANES, (c + 1) * LANES)

    def bias_lane(h, piece):
        return lane == (HEAD_DIM * (1 - h) + piece)

    @pl.when(i == 0)
    def _():
        for c in range(tiles):
            def write(pos, vt, c=c):
                for h in range(2):
                    vt_ref[2 * c + h, 0:HEAD_DIM, pos] = vt[h * HEAD_DIM:(h + 1) * HEAD_DIM]

            _for_each_transposed_chunk(v_ref, tile(c), eye_ref, write)
            step = min(TRANSPOSE_ROWS, k_ref.shape[1])
            for start in range(0, k_ref.shape[1], step):
                pos = slice(start, start + step)
                k = k_ref[0, pos, tile(c)]
                pieces = _split_bf16(nc_ref[0, pos, tile(c)], BIAS_PIECES)
                for h in range(2):
                    extra = jnp.zeros_like(k)
                    for n, piece in enumerate(pieces):
                        extra = jnp.where(bias_lane(h, n), piece, extra)
                    kx_ref[2 * c + h, pos, :] = jnp.where(low == (h == 0), k, extra)
        vt_ref[:, HEAD_DIM:, :] = jnp.ones((2 * tiles, ONES_ROWS, vt_ref.shape[2]), BF16)

    qs = []
    for c in range(tiles):
        q = q_ref[0, :, tile(c)]
        for h in range(2):
            ones = functools.reduce(jnp.logical_or, [bias_lane(h, n) for n in range(BIAS_PIECES)])
            qs.append(jnp.where(low == (h == 0), q, ones.astype(BF16)))

    def scores(kb):
        return tuple(_dot_nt(kx_ref[st, _block_ds(kb, tq), :], qs[st]) for st in range(2 * tiles))

    def values(kb):
        return tuple(vt_ref[st, :, _block_ds(kb, tq)] for st in range(2 * tiles))

    causal = _block_iota(tq, 0) <= _block_iota(tq, 1)
    diag = tuple(jnp.where(causal, s, -jnp.inf) for s in scores(i))
    acc = _softmax_walk(i, diag, scores, values, s_ref, p_ref, acc_ref, stat_ref, tq)
    outs = [a[0:HEAD_DIM] / a[HEAD_DIM:HEAD_DIM + 1] for a in acc]
    for c in range(tiles):
        o_t = jnp.concatenate(outs[2 * c:2 * c + 2], axis=0).astype(BF16)
        o_ref[0, :, tile(c)] = _mxu_transpose(o_t, eye_ref)


def _attn_fox(yb, negcum, eye, tq, tiles):
    b, s, _ = yb.shape
    groups = N_PAIR_FOX // tiles
    width = tiles * LANES
    return pl.pallas_call(
        functools.partial(_attn_fox_kernel, tq=tq),
        grid=(b, groups, s // tq),
        in_specs=[pl.BlockSpec((1, tq, width), lambda bi, j, i: (bi, i, j)),
                  pl.BlockSpec((1, s, width), lambda bi, j, i: (bi, 0, groups + j)),
                  pl.BlockSpec((1, s, width), lambda bi, j, i: (bi, 0, 2 * groups + j)),
                  pl.BlockSpec((1, s, width), lambda bi, j, i: (bi, 0, j)), _const_spec(eye.shape)],
        out_specs=pl.BlockSpec((1, tq, width), lambda bi, j, i: (bi, i, j)),
        out_shape=jax.ShapeDtypeStruct((b, s, W_FOX), BF16),
        scratch_shapes=[pltpu.VMEM((2 * tiles, s, LANES), BF16),
                        pltpu.VMEM((2 * tiles, HEAD_DIM + ONES_ROWS, s), BF16)]
        + _softmax_scratch(2 * tiles, tq, HEAD_DIM + ONES_ROWS),
        compiler_params=_params(3),
        name="attn_forget",
    )(yb, yb, yb, negcum, eye)


def _attn_diff_kernel(q_ref, k_ref, v_ref, lam_ref, subln_ref, eye_ref, o_ref, vt_ref, s_ref, p_ref, acc_ref,
                      stat_ref, *, tq, lam_init):
    i = pl.program_id(2)
    tiles = q_ref.shape[2] // LANES

    def tile(c):
        return slice(c * LANES, (c + 1) * LANES)

    @pl.when(i == 0)
    def _():
        for c in range(tiles):
            def write(pos, vt, c=c):
                vt_ref[c, 0:LANES, pos] = vt

            _for_each_transposed_chunk(v_ref, tile(c), eye_ref, write)
        vt_ref[:, LANES:, :] = jnp.ones((tiles, ONES_ROWS, vt_ref.shape[2]), BF16)

    qs = [qh for c in range(tiles) for qh in _split_heads(q_ref[0, :, tile(c)])]

    def scores(kb):
        out = []
        for c in range(tiles):
            k = k_ref[0, _block_ds(kb, tq), tile(c)]
            out += [_dot_nt(k, qs[2 * c + h]) for h in range(2)]
        return tuple(out)

    def values(kb):
        out = []
        for c in range(tiles):
            vt = vt_ref[c, :, _block_ds(kb, tq)]
            out += [vt, vt]
        return tuple(out)

    visible = (_block_iota(tq, 0) // CHUNK) <= (_block_iota(tq, 1) // CHUNK)
    diag = tuple(jnp.where(visible, s, -jnp.inf) for s in scores(i))
    acc = _softmax_walk(i, diag, scores, values, s_ref, p_ref, acc_ref, stat_ref, tq)

    lf = lam_ref[...]
    lam = (jnp.exp(jnp.sum(lf[0:1] * lf[1:2], axis=1, keepdims=True))
           - jnp.exp(jnp.sum(lf[2:3] * lf[3:4], axis=1, keepdims=True)) + lam_init)
    for c in range(tiles):
        a1, a2 = (a[0:LANES] / a[LANES:LANES + 1] for a in acc[2 * c:2 * c + 2])
        o_t = a1 - lam * a2
        ms = jnp.mean(o_t * o_t, axis=0, keepdims=True)
        o_t = o_t * lax.rsqrt(ms + EPS) * subln_ref[...] * (1.0 - lam_init)
        o_ref[0, :, tile(c)] = _mxu_transpose(o_t.astype(BF16), eye_ref)


def _attn_diff(yc, diff_lambda, subln, eye, tq, tiles, lam_init):
    b, s, _ = yc.shape
    groups = H_DIFF // tiles
    width = tiles * LANES
    return pl.pallas_call(
        functools.partial(_attn_diff_kernel, tq=tq, lam_init=lam_init),
        grid=(b, groups, s // tq),
        in_specs=[pl.BlockSpec((1, tq, width), lambda bi, j, i: (bi, i, j)),
                  pl.BlockSpec((1, s, width), lambda bi, j, i: (bi, 0, groups + j)),
                  pl.BlockSpec((1, s, width), lambda bi, j, i: (bi, 0, 2 * groups + j)),
                  _const_spec(diff_lambda.shape), _const_spec(subln.shape), _const_spec(eye.shape)],
        out_specs=pl.BlockSpec((1, tq, width), lambda bi, j, i: (bi, i, j)),
        out_shape=jax.ShapeDtypeStruct((b, s, W_DIFF), BF16),
        scratch_shapes=[pltpu.VMEM((tiles, LANES + ONES_ROWS, s), BF16)]
        + _softmax_scratch(2 * tiles, tq, LANES + ONES_ROWS),
        compiler_params=_params(3),
        name="attn_diff",
    )(yc, yc, yc, diff_lambda, subln, eye)


def _pair_tile(x):
    return jnp.concatenate([x, x], axis=-1)


def kernel(x, p, positions, ffn1_norm, ffn1_wi, ffn1_wo, mix_norm, w_in, b_forget, qk_gain_fox, qk_gain_diff, diff_lambda, diff_subln, w_br, w_o, ffn2_norm, ffn2_wi, ffn2_wo, ple_norm, ple_gate_w, ple_proj_w):
    b, s, d = x.shape
    depth = p.shape[0]
    t = b * s
    tm = min(512, t)
    tq = min(ATTN_BLOCK, s)
    n_chunks = D_FF // FF_CHUNK

    idx = jnp.arange(LANES)
    grp = jnp.where((idx[:, None] // HEAD_DIM) == (idx[None, :] // HEAD_DIM), 1.0 / HEAD_DIM, 0.0).astype(BF16)
    tri_incl = (idx[:, None] >= idx[None, :]).astype(BF16)
    eye = jnp.eye(max(tq, LANES), dtype=BF16)
    later = (idx[None, :] > idx[:, None]).astype(BF16)
    lu = jnp.concatenate([jnp.concatenate([later, later], axis=1), jnp.ones((ONES_ROWS, 2 * LANES), BF16)], axis=0)

    rc, rs1, rs2 = _rope_tables(positions, tm)

    def ffn_weights(wi, wo):
        wa = wi[:, :D_FF].reshape(d, n_chunks, FF_CHUNK).transpose(1, 0, 2).astype(BF16)
        wg = wi[:, D_FF:].reshape(d, n_chunks, FF_CHUNK).transpose(1, 0, 2).astype(BF16)
        return wa, wg, wo.reshape(n_chunks, FF_CHUNK, d).astype(BF16)

    fb_col = jnp.arange(FB_COLS)
    fb_head = 2 * (fb_col // LANES) + (fb_col % LANES < HEAD_DIM)
    h = x.reshape(t, d)
    for i in range(depth):
        lam_init = 0.8 - 0.6 * math.exp(-0.3 * i)
        o = IN_OFFSETS
        fb_w = w_in[i][:, o["fb"]:o["c"]][:, fb_head]
        w_proj = jnp.concatenate([w_in[i][:, o["a"]:o["fb"]], w_in[i][:, o["c"]:o["gates"]], fb_w], axis=1).astype(BF16)
        w_gate = w_in[i][:, o["gates"]:o["end"]].astype(BF16)
        fb_bias = b_forget[i][fb_head][None]

        h = _ffn(h, ffn1_norm[i][None], *ffn_weights(ffn1_wi[i], ffn1_wo[i]), tm)
        ya, yb, yc, fb = _inproj(h, mix_norm[i][None], w_proj, _pair_tile(qk_gain_fox[i]),
                                 _pair_tile(qk_gain_diff[i]), grp, rc, rs1, rs2, tm)
        negcum = _forget_cumsum(fb.reshape(b, s, FB_COLS), fb_bias, tri_incl)
        oa = _attn_sb(ya.reshape(b, s, -1), lu, eye, tq, SB_PAIRS)
        ob = _attn_fox(yb.reshape(b, s, -1), negcum, eye, tq, FOX_TILES)
        oc = _attn_diff(yc.reshape(b, s, -1), diff_lambda[i], diff_subln[i][:, None], eye, tq, DIFF_TILES, lam_init)
        h = _merge(h, oa.reshape(t, -1), ob.reshape(t, -1), oc.reshape(t, -1), mix_norm[i][None],
                   w_gate, w_br[i].astype(BF16), w_o[i].astype(BF16), tm)
        h = _ffn(h, ffn2_norm[i][None], *ffn_weights(ffn2_wi[i], ffn2_wo[i]), tm)
        h = _ple(h, p[i].reshape(t, P_DIM), ple_norm[i][None], ple_gate_w[i].astype(BF16),
                 ple_proj_w[i].astype(BF16), tm)
    return h.reshape(b, s, d)
```

```python
import functools
import math

import jax
import jax.numpy as jnp
from jax import lax
from jax.experimental import pallas as pl
from jax.experimental.pallas import tpu as pltpu

D_MODEL = 1024
HEAD_DIM = 64
LANES = 128
N_PAIR_SB = 2
N_PAIR_FOX = 2
H_DIFF = 4
W_SB = 256
W_FOX = 256
W_DIFF = 512
ROT_DIM = 16
ROPE_THETA = 500000.0
D_FF = 2816
FF_CHUNK = 256
ATTN_BLOCK = 256
SB_PAIRS = 2
FOX_TILES = 2
DIFF_TILES = 4
P_DIM = 256
EPS = 1e-6
CHUNK = 64
LOG2E = math.log2(math.e)
SCALE = HEAD_DIM ** -0.5 * LOG2E
FB_COLS = 2 * N_PAIR_FOX * LANES
IN_OFFSETS = {"a": 0, "b": 768, "fb": 1536, "c": 1540, "gates": 3076, "end": 6148}
VMEM_LIMIT = 56 * 1024 * 1024

F32 = jnp.float32
BF16 = jnp.bfloat16
NT_DIMS = (((1,), (1,)), ((), ()))


def _dot(a, b):
    return jnp.dot(a, b, preferred_element_type=F32)


def _dot_nt(a, b):
    return lax.dot_general(a, b, NT_DIMS, preferred_element_type=F32)


def _rms(x, gain):
    ms = jnp.mean(x * x, axis=-1, keepdims=True)
    return x * lax.rsqrt(ms + EPS) * gain


def _split_bf16(x, parts):
    out = []
    for _ in range(parts - 1):
        bits = lax.bitcast_convert_type(x, jnp.uint32) & jnp.uint32(0xFFFF0000)
        piece = lax.bitcast_convert_type(bits, F32)
        out.append(piece.astype(BF16))
        x = x - piece
    out.append(x.astype(BF16))
    return out


def _neg_abs(x):
    bits = lax.bitcast_convert_type(x, jnp.uint32) | jnp.uint32(0x80000000)
    return lax.bitcast_convert_type(bits, F32)


def _params(n_grid):
    return pltpu.CompilerParams(
        dimension_semantics=("arbitrary",) * n_grid, vmem_limit_bytes=VMEM_LIMIT)


def _const_spec(shape):
    zeros = (0,) * len(shape)
    return pl.BlockSpec(shape, lambda *_: zeros, pipeline_mode=pl.Buffered(1))


def _ffn_kernel(h_ref, gain_ref, wa_ref, wg_ref, wo_ref, o_ref, acc_ref):
    h = h_ref[...]
    xn = _rms(h, gain_ref[...]).astype(BF16)
    acc_ref[...] = jnp.zeros_like(acc_ref)

    def chunk(c, carry):
        a = _dot(xn, wa_ref[c])
        g = _dot(xn, wg_ref[c])
        act = (a * jax.nn.sigmoid(a) * g).astype(BF16)
        acc_ref[...] += _dot(act, wo_ref[c])
        return carry

    lax.fori_loop(0, wa_ref.shape[0], chunk, 0, unroll=True)
    o_ref[...] = h + 0.5 * acc_ref[...]


def _ffn(h, gain, wa, wg, wo, tm):
    t = h.shape[0]
    row = pl.BlockSpec((tm, D_MODEL), lambda i: (i, 0))
    return pl.pallas_call(
        _ffn_kernel,
        grid=(t // tm,),
        in_specs=[row, _const_spec(gain.shape), _const_spec(wa.shape), _const_spec(wg.shape),
                  _const_spec(wo.shape)],
        out_specs=row,
        out_shape=jax.ShapeDtypeStruct(h.shape, F32),
        scratch_shapes=[pltpu.VMEM((tm, D_MODEL), F32)],
        compiler_params=_params(1),
        name="ffn",
    )(h, gain, wa, wg, wo)


def _rope_kernel(pos_ref, const_ref, c_ref, s1_ref, s2_ref):
    ang = pos_ref[...].astype(F32) * const_ref[0:1, :]
    sin = jnp.sin(ang)
    c_ref[...] = jnp.cos(ang)
    s1_ref[...] = sin * const_ref[1:2, :]
    s2_ref[...] = sin * const_ref[2:3, :]


def _rope_tables(positions, tm):
    t = positions.size
    lane = jnp.arange(LANES) % HEAD_DIM
    inv_freq = ROPE_THETA ** (-jnp.arange(0, ROT_DIM, 2, dtype=F32) / ROT_DIM)
    half = ROT_DIM // 2
    freq_lane = jnp.where(lane < ROT_DIM, inv_freq[lane % half], 0.0)
    plus = jnp.where((lane >= half) & (lane < ROT_DIM), 1.0, 0.0)
    minus = jnp.where(lane < half, -1.0, 0.0)
    const = jnp.zeros((8, LANES), F32).at[0].set(freq_lane).at[1].set(plus).at[2].set(minus)
    out = jax.ShapeDtypeStruct((t, LANES), F32)
    row = pl.BlockSpec((tm, LANES), lambda i: (i, 0))
    return pl.pallas_call(
        _rope_kernel,
        grid=(t // tm,),
        in_specs=[pl.BlockSpec((tm, 1), lambda i: (i, 0)), _const_spec(const.shape)],
        out_specs=[row, row, row],
        out_shape=[out, out, out],
        compiler_params=_params(1),
        name="rope_tables",
    )(positions.reshape(t, 1), const)


def _inproj_kernel(h_ref, gain_ref, w_ref, gfox_ref, gdiff_ref, grp_ref, rc_ref, rs1_ref, rs2_ref,
                   ya_ref, yb_ref, yc_ref, fb_ref):
    xn = _rms(h_ref[...], gain_ref[...]).astype(BF16)
    grp = grp_ref[...]

    def head_norm(x, gain):
        ms = _dot((x * x).astype(BF16), grp)
        return x * lax.rsqrt(ms + EPS) * gain

    def rope(x):
        return (x * rc_ref[...] + pltpu.roll(x, ROT_DIM // 2, 1) * rs1_ref[...]
                + pltpu.roll(x, LANES - ROT_DIM // 2, 1) * rs2_ref[...])

    y = _dot(xn, w_ref[:, 0:768])
    ya_ref[:, 0:W_SB] = (y[:, 0:W_SB] * SCALE).astype(BF16)
    ya_ref[:, W_SB:768] = y[:, W_SB:768].astype(BF16)

    y = _dot(xn, w_ref[:, 768:1536])
    for blk in range(2 * N_PAIR_FOX):
        is_q = blk < N_PAIR_FOX
        x = head_norm(y[:, blk * LANES:(blk + 1) * LANES], gfox_ref[(0 if is_q else 1):(1 if is_q else 2), :])
        if is_q:
            x = x * SCALE
        yb_ref[:, blk * LANES:(blk + 1) * LANES] = x.astype(BF16)
    yb_ref[:, 2 * W_FOX:768] = y[:, 2 * W_FOX:768].astype(BF16)

    y = _dot(xn, w_ref[:, 1536:3072])
    for blk in range(2 * H_DIFF):
        is_q = blk < H_DIFF
        x = head_norm(y[:, blk * LANES:(blk + 1) * LANES], gdiff_ref[(0 if is_q else 1):(1 if is_q else 2), :])
        x = rope(x)
        if is_q:
            x = x * SCALE
        yc_ref[:, blk * LANES:(blk + 1) * LANES] = x.astype(BF16)
    yc_ref[:, 2 * W_DIFF:1536] = y[:, 2 * W_DIFF:1536].astype(BF16)

    fb_ref[...] = _dot(xn, w_ref[:, 3072:3072 + FB_COLS])


def _inproj(h, gain, w, gfox, gdiff, grp, rc, rs1, rs2, tm):
    t = h.shape[0]
    row = lambda n: pl.BlockSpec((tm, n), lambda i: (i, 0))
    return pl.pallas_call(
        _inproj_kernel,
        grid=(t // tm,),
        in_specs=[row(D_MODEL), _const_spec(gain.shape), _const_spec(w.shape), _const_spec(gfox.shape),
                  _const_spec(gdiff.shape), _const_spec(grp.shape), row(LANES), row(LANES), row(LANES)],
        out_specs=[row(768), row(768), row(1536), row(FB_COLS)],
        out_shape=[jax.ShapeDtypeStruct((t, 768), BF16), jax.ShapeDtypeStruct((t, 768), BF16),
                   jax.ShapeDtypeStruct((t, 1536), BF16), jax.ShapeDtypeStruct((t, FB_COLS), F32)],
        compiler_params=_params(1),
        name="inproj",
    )(h, gain, w, gfox, gdiff, grp, rc, rs1, rs2)


def _forget_kernel(fb_ref, bias_ref, tri_ref, o_ref):
    s = fb_ref.shape[1]
    tri = tri_ref[...]
    carry = jnp.zeros((1, fb_ref.shape[2]), F32)
    for blk in range(s // LANES):
        x = fb_ref[0, blk * LANES:(blk + 1) * LANES, :] + bias_ref[...]
        log_f = jnp.minimum(x, 0.0) - jnp.log(1.0 + jnp.exp(-jnp.abs(x)))
        cum = carry
        for piece in _split_bf16(log_f, 3):
            cum = cum + _dot(tri, piece)
        o_ref[0, blk * LANES:(blk + 1) * LANES, :] = cum * -LOG2E
        carry = cum[LANES - 1:LANES, :]


def _forget_cumsum(fb, bias, tri):
    b, s, cols = fb.shape
    spec = pl.BlockSpec((1, s, cols), lambda i: (i, 0, 0))
    return pl.pallas_call(
        _forget_kernel,
        grid=(b,),
        in_specs=[spec, _const_spec(bias.shape), _const_spec(tri.shape)],
        out_specs=spec,
        out_shape=jax.ShapeDtypeStruct(fb.shape, F32),
        compiler_params=_params(1),
        name="forget_cumsum",
    )(fb, bias, tri)


def _merge_kernel(h_ref, oa_ref, ob_ref, oc_ref, gain_ref, wgate_ref, wbr_ref, wo_ref, o_ref):
    h = h_ref[...]
    u = _rms(h, gain_ref[...]).astype(BF16)
    merged = None
    row0 = 0
    for br, o_br in enumerate((oa_ref, ob_ref, oc_ref)):
        width = o_br.shape[1]
        y = _dot(o_br[...], wbr_ref[row0:row0 + width, :])
        gate = jax.nn.sigmoid(_dot(u, wgate_ref[:, br * D_MODEL:(br + 1) * D_MODEL]))
        merged = gate * y if merged is None else merged + gate * y
        row0 += width
    o_ref[...] = h + _dot(merged.astype(BF16), wo_ref[...])


def _merge(h, oa, ob, oc, gain, wgate, wbr, wo, tm):
    t = h.shape[0]
    row = lambda n: pl.BlockSpec((tm, n), lambda i: (i, 0))
    return pl.pallas_call(
        _merge_kernel,
        grid=(t // tm,),
        in_specs=[row(D_MODEL), row(W_SB), row(W_FOX), row(W_DIFF), _const_spec(gain.shape),
                  _const_spec(wgate.shape), _const_spec(wbr.shape), _const_spec(wo.shape)],
        out_specs=row(D_MODEL),
        out_shape=jax.ShapeDtypeStruct(h.shape, F32),
        compiler_params=_params(1),
        name="merge",
    )(h, oa, ob, oc, gain, wgate, wbr, wo)


def _ple_kernel(h_ref, p_ref, gain_ref, wgate_ref, wproj_ref, o_ref):
    h = h_ref[...]
    u = _rms(h, gain_ref[...]).astype(BF16)
    gate = jax.nn.sigmoid(_dot(u, wgate_ref[...]))
    o_ref[...] = h + gate * _dot(p_ref[...].astype(BF16), wproj_ref[...])


def _ple(h, p, gain, wgate, wproj, tm):
    t = h.shape[0]
    row = lambda n: pl.BlockSpec((tm, n), lambda i: (i, 0))
    return pl.pallas_call(
        _ple_kernel,
        grid=(t // tm,),
        in_specs=[row(D_MODEL), row(P_DIM), _const_spec(gain.shape), _const_spec(wgate.shape),
                  _const_spec(wproj.shape)],
        out_specs=row(D_MODEL),
        out_shape=jax.ShapeDtypeStruct(h.shape, F32),
        compiler_params=_params(1),
        name="ple",
    )(h, p, gain, wgate, wproj)


TRANSPOSE_ROWS = 512
NO_BLOCK_LOGIT = -1e30


def _low_lanes():
    return lax.broadcasted_iota(jnp.int32, (1, LANES), 1) < HEAD_DIM


def _block_iota(tq, axis):
    return lax.broadcasted_iota(jnp.int32, (tq, tq), axis)


def _block_ds(kb, tq):
    if isinstance(kb, int):
        return pl.ds(kb * tq, tq)
    return pl.ds(pl.multiple_of(kb * tq, tq), tq)


def _mxu_transpose(x, eye_ref):
    n = x.shape[1]
    return _dot_nt(eye_ref[0:n, 0:n], x).astype(BF16)


def _for_each_transposed_chunk(src_ref, lanes, eye_ref, write):
    s = src_ref.shape[1]
    step = min(TRANSPOSE_ROWS, s)
    for start in range(0, s, step):
        pos = slice(start, start + step)
        write(pos, _mxu_transpose(src_ref[0, pos, lanes], eye_ref))


def _split_heads(q):
    low = _low_lanes()
    zero = jnp.zeros_like(q)
    return jnp.where(low, q, zero), jnp.where(low, zero, q)


def _attn_sb_kernel(q_ref, k_ref, v_ref, lu_ref, eye_ref, o_ref, vt_ref, lb_ref, lk_ref, a_ref, acc_ref, run_ref,
                    *, tq):
    i = pl.program_id(2)
    pairs = q_ref.shape[2] // LANES

    def tile(c):
        return slice(c * LANES, (c + 1) * LANES)

    @pl.when(i == 0)
    def _():
        for c in range(pairs):
            def write(pos, vt, c=c):
                vt_ref[tile(c), pos] = vt

            _for_each_transposed_chunk(v_ref, tile(c), eye_ref, write)

    qs = [qh for c in range(pairs) for qh in _split_heads(q_ref[0, :, tile(c)])]
    strict = _block_iota(tq, 0) < _block_iota(tq, 1)
    lu = lu_ref[...]
    heads = range(2 * pairs)
    n_tiles = tq // LANES

    def logits(kb):
        return [_dot_nt(k_ref[0, _block_ds(kb, tq), tile(h // 2)], qs[h]) for h in heads]

    def logits_stage(zs, slot, diag):
        for h in heads:
            z = zs[h]
            log_beta = jnp.minimum(z, 0.0) - jnp.log2(1.0 + jnp.exp2(_neg_abs(z)))
            log_keep = log_beta - z
            if diag:
                log_keep = jnp.where(strict, log_keep, 0.0)
                log_beta = jnp.where(strict, log_beta, -jnp.inf)
            lb_ref[slot, h] = log_beta
            hi, lo = _split_bf16(log_keep, 2)
            for c in range(n_tiles):
                lk_ref[slot, h, 2 * c * LANES:(2 * c + 1) * LANES, :] = hi[c * LANES:(c + 1) * LANES, :]
                lk_ref[slot, h, (2 * c + 1) * LANES:(2 * c + 2) * LANES, :] = lo[c * LANES:(c + 1) * LANES, :]

    def suffix_sums(slot):
        return [[_dot(lu, lk_ref[slot, h, 2 * c * LANES:(2 * c + 2) * LANES, :]) for c in range(n_tiles)]
                for h in heads]

    def weights_stage(slot, h, sums, run):
        for c in reversed(range(n_tiles)):
            sl = slice(c * LANES, (c + 1) * LANES)
            a_ref[slot, h, sl, :] = jnp.exp2(lb_ref[slot, h, sl, :] + sums[c][0:LANES, :] + run).astype(BF16)
            run = run + sums[c][LANES:LANES + 1, :]
        return run

    def products(slot, kb):
        return [_dot(vt_ref[h * HEAD_DIM:(h + 1) * HEAD_DIM, _block_ds(kb, tq)], a_ref[slot, h]) for h in heads]

    def accumulate(pv):
        for h in heads:
            acc_ref[h] += pv[h]

    def step(n, cur):
        kb = i - n
        zs = logits(jnp.maximum(kb - 1, 0))
        if cur == 0:
            zs = [jnp.where(kb >= 1, z, NO_BLOCK_LOGIT) for z in zs]
        sums = suffix_sums(cur)
        pv = products(1 - cur, jnp.minimum(kb + 1, i))
        for h in heads:
            run_ref[h] = weights_stage(cur, h, sums[h], run_ref[h])
        logits_stage(zs, 1 - cur, False)
        accumulate(pv)

    def two_steps(t, carry):
        step(2 * t, 0)
        step(2 * t + 1, 1)
        return carry

    logits_stage(logits(i), 0, True)
    a_ref[1] = jnp.zeros(a_ref.shape[1:], BF16)
    acc_ref[...] = jnp.zeros_like(acc_ref)
    run_ref[...] = jnp.zeros_like(run_ref)
    lax.fori_loop(0, i // 2 + 1, two_steps, 0)
    accumulate(products(1, 0))
    for c in range(pairs):
        o_t = jnp.concatenate([acc_ref[2 * c], acc_ref[2 * c + 1]], axis=0).astype(BF16)
        o_ref[0, :, tile(c)] = _mxu_transpose(o_t, eye_ref)


def _attn_sb(ya, lu, eye, tq, pairs):
    b, s, _ = ya.shape
    groups = N_PAIR_SB // pairs
    width = pairs * LANES
    heads = 2 * pairs
    return pl.pallas_call(
        functools.partial(_attn_sb_kernel, tq=tq),
        grid=(b, groups, s // tq),
        in_specs=[pl.BlockSpec((1, tq, width), lambda bi, j, i: (bi, i, j)),
                  pl.BlockSpec((1, s, width), lambda bi, j, i: (bi, 0, groups + j)),
                  pl.BlockSpec((1, s, width), lambda bi, j, i: (bi, 0, 2 * groups + j)),
                  _const_spec(lu.shape), _const_spec(eye.shape)],
        out_specs=pl.BlockSpec((1, tq, width), lambda bi, j, i: (bi, i, j)),
        out_shape=jax.ShapeDtypeStruct((b, s, W_SB), BF16),
        scratch_shapes=[pltpu.VMEM((width, s), BF16),
                        pltpu.VMEM((2, heads, tq, tq), F32), pltpu.VMEM((2, heads, 2 * tq, tq), BF16),
                        pltpu.VMEM((2, heads, tq, tq), BF16), pltpu.VMEM((heads, HEAD_DIM, tq), F32),
                        pltpu.VMEM((heads, 1, tq), F32)],
        compiler_params=_params(3),
        name="attn_stickbreak",
    )(ya, ya, ya, lu, eye)


def _softmax_scratch(n_streams, tq, acc_rows):
    return [pltpu.VMEM((2, n_streams, tq, tq), F32), pltpu.VMEM((2, n_streams, tq, tq), BF16),
            pltpu.VMEM((n_streams, acc_rows, tq), F32), pltpu.VMEM((2, n_streams, 1, tq), F32)]


def _softmax_walk(i, diag_scores, scores, values, s_ref, p_ref, acc_ref, stat_ref, tq):
    n_st = len(diag_scores)
    streams = range(n_st)
    q_tiles = range(tq // LANES)

    def products(slot, kb):
        v = values(kb)
        return [_dot(v[st], p_ref[slot, st]) for st in streams]

    def accumulate(alpha, pv):
        for st in streams:
            acc_ref[st] = acc_ref[st] * alpha[st] + pv[st]

    def step(n, cur):
        kb = i - n
        s_next = scores(jnp.maximum(kb - 1, 0))
        if cur == 0:
            s_next = [jnp.where(kb >= 1, s, -jnp.inf) for s in s_next]
        pv = products(1 - cur, jnp.minimum(kb + 1, i))
        alpha_prev = [stat_ref[1, st] for st in streams]
        for st in streams:
            for j in q_tiles:
                cols = slice(j * LANES, (j + 1) * LANES)
                s = s_ref[cur, st, :, cols]
                m_old = stat_ref[0, st, :, cols]
                m_j = jnp.maximum(m_old, jnp.max(s, axis=0, keepdims=True))
                p_ref[cur, st, :, cols] = jnp.exp2(s - m_j).astype(BF16)
                stat_ref[0, st, :, cols] = m_j
                stat_ref[1, st, :, cols] = jnp.exp2(m_old - m_j)
        for st in streams:
            s_ref[1 - cur, st] = s_next[st]
        accumulate(alpha_prev, pv)

    def two_steps(t, carry):
        step(2 * t, 0)
        step(2 * t + 1, 1)
        return carry

    for st in streams:
        s_ref[0, st] = diag_scores[st]
    p_ref[1] = jnp.zeros(p_ref.shape[1:], BF16)
    acc_ref[...] = jnp.zeros_like(acc_ref)
    stat_ref[0] = jnp.full(stat_ref.shape[1:], -jnp.inf, F32)
    stat_ref[1] = jnp.ones(stat_ref.shape[1:], F32)
    lax.fori_loop(0, i // 2 + 1, two_steps, 0)
    accumulate([stat_ref[1, st] for st in streams], products(1, 0))
    return [acc_ref[st] for st in streams]


ONES_ROWS = 16


def _attn_fox_kernel(q_ref, k_ref, v_ref, nc_ref, eye_ref, o_ref, vt_ref, s_ref, p_ref, acc_ref, stat_ref, *,
                     tq):
    i = pl.program_id(2)
    tiles = q_ref.shape[2] // LANES

    def tile(c):
        return slice(c * LANES, (c + 1) * LANES)

    @pl.when(i == 0)
    def _():
        for c in range(tiles):
            def write(pos, vt, c=c):
                for h in range(2):
                    vt_ref[2 * c + h, 0:HEAD_DIM, pos] = vt[h * HEAD_DIM:(h + 1) * HEAD_DIM]

            _for_each_transposed_chunk(v_ref, tile(c), eye_ref, write)
        vt_ref[:, HEAD_DIM:, :] = jnp.ones((2 * tiles, ONES_ROWS, vt_ref.shape[2]), BF16)

    qs = [qh for c in range(tiles) for qh in _split_heads(q_ref[0, :, tile(c)])]

    def scores(kb):
        out = []
        for st in range(2 * tiles):
            bias = nc_ref[0, _block_ds(kb, tq), tile(st)]
            s = _dot_nt(k_ref[0, _block_ds(kb, tq), tile(st // 2)], qs[st])
            out.append(s + jnp.concatenate([bias] * (tq // LANES), axis=1))
        return tuple(out)

    def values(kb):
        return tuple(vt_ref[st, :, _block_ds(kb, tq)] for st in range(2 * tiles))

    causal = _block_iota(tq, 0) <= _block_iota(tq, 1)
    diag = tuple(jnp.where(causal, s, -jnp.inf) for s in scores(i))
    acc = _softmax_walk(i, diag, scores, values, s_ref, p_ref, acc_ref, stat_ref, tq)
    outs = [a[0:HEAD_DIM] / a[HEAD_DIM:HEAD_DIM + 1] for a in acc]
    for c in range(tiles):
        o_t = jnp.concatenate(outs[2 * c:2 * c + 2], axis=0).astype(BF16)
        o_ref[0, :, tile(c)] = _mxu_transpose(o_t, eye_ref)


def _attn_fox(yb, negcum, eye, tq, tiles):
    b, s, _ = yb.shape
    groups = N_PAIR_FOX // tiles
    width = tiles * LANES
    return pl.pallas_call(
        functools.partial(_attn_fox_kernel, tq=tq),
        grid=(b, groups, s // tq),
        in_specs=[pl.BlockSpec((1, tq, width), lambda bi, j, i: (bi, i, j)),
                  pl.BlockSpec((1, s, width), lambda bi, j, i: (bi, 0, groups + j)),
                  pl.BlockSpec((1, s, width), lambda bi, j, i: (bi, 0, 2 * groups + j)),
                  pl.BlockSpec((1, s, 2 * width), lambda bi, j, i: (bi, 0, j)), _const_spec(eye.shape)],
        out_specs=pl.BlockSpec((1, tq, width), lambda bi, j, i: (bi, i, j)),
        out_shape=jax.ShapeDtypeStruct((b, s, W_FOX), BF16),
        scratch_shapes=[pltpu.VMEM((2 * tiles, HEAD_DIM + ONES_ROWS, s), BF16)]
        + _softmax_scratch(2 * tiles, tq, HEAD_DIM + ONES_ROWS),
        compiler_params=_params(3),
        name="attn_forget",
    )(yb, yb, yb, negcum, eye)


def _attn_diff_kernel(q_ref, k_ref, v_ref, lam_ref, subln_ref, eye_ref, o_ref, vt_ref, s_ref, p_ref, acc_ref,
                      stat_ref, *, tq, lam_init):
    i = pl.program_id(2)
    tiles = q_ref.shape[2] // LANES

    def tile(c):
        return slice(c * LANES, (c + 1) * LANES)

    @pl.when(i == 0)
    def _():
        for c in range(tiles):
            def write(pos, vt, c=c):
                vt_ref[c, 0:LANES, pos] = vt

            _for_each_transposed_chunk(v_ref, tile(c), eye_ref, write)
        vt_ref[:, LANES:, :] = jnp.ones((tiles, ONES_ROWS, vt_ref.shape[2]), BF16)

    qs = [qh for c in range(tiles) for qh in _split_heads(q_ref[0, :, tile(c)])]

    def scores(kb):
        out = []
        for c in range(tiles):
            k = k_ref[0, _block_ds(kb, tq), tile(c)]
            out += [_dot_nt(k, qs[2 * c + h]) for h in range(2)]
        return tuple(out)

    def values(kb):
        out = []
        for c in range(tiles):
            vt = vt_ref[c, :, _block_ds(kb, tq)]
            out += [vt, vt]
        return tuple(out)

    visible = (_block_iota(tq, 0) // CHUNK) <= (_block_iota(tq, 1) // CHUNK)
    diag = tuple(jnp.where(visible, s, -jnp.inf) for s in scores(i))
    acc = _softmax_walk(i, diag, scores, values, s_ref, p_ref, acc_ref, stat_ref, tq)

    lf = lam_ref[...]
    lam = (jnp.exp(jnp.sum(lf[0:1] * lf[1:2], axis=1, keepdims=True))
           - jnp.exp(jnp.sum(lf[2:3] * lf[3:4], axis=1, keepdims=True)) + lam_init)
    for c in range(tiles):
        a1, a2 = (a[0:LANES] / a[LANES:LANES + 1] for a in acc[2 * c:2 * c + 2])
        o_t = a1 - lam * a2
        ms = jnp.mean(o_t * o_t, axis=0, keepdims=True)
        o_t = o_t * lax.rsqrt(ms + EPS) * subln_ref[...] * (1.0 - lam_init)
        o_ref[0, :, tile(c)] = _mxu_transpose(o_t.astype(BF16), eye_ref)


def _attn_diff(yc, diff_lambda, subln, eye, tq, tiles, lam_init):
    b, s, _ = yc.shape
    groups = H_DIFF // tiles
    width = tiles * LANES
    return pl.pallas_call(
        functools.partial(_attn_diff_kernel, tq=tq, lam_init=lam_init),
        grid=(b, groups, s // tq),
        in_specs=[pl.BlockSpec((1, tq, width), lambda bi, j, i: (bi, i, j)),
                  pl.BlockSpec((1, s, width), lambda bi, j, i: (bi, 0, groups + j)),
                  pl.BlockSpec((1, s, width), lambda bi, j, i: (bi, 0, 2 * groups + j)),
                  _const_spec(diff_lambda.shape), _const_spec(subln.shape), _const_spec(eye.shape)],
        out_specs=pl.BlockSpec((1, tq, width), lambda bi, j, i: (bi, i, j)),
        out_shape=jax.ShapeDtypeStruct((b, s, W_DIFF), BF16),
        scratch_shapes=[pltpu.VMEM((tiles, LANES + ONES_ROWS, s), BF16)]
        + _softmax_scratch(2 * tiles, tq, LANES + ONES_ROWS),
        compiler_params=_params(3),
        name="attn_diff",
    )(yc, yc, yc, diff_lambda, subln, eye)


def _pair_tile(x):
    return jnp.concatenate([x, x], axis=-1)


def kernel(x, p, positions, ffn1_norm, ffn1_wi, ffn1_wo, mix_norm, w_in, b_forget, qk_gain_fox, qk_gain_diff, diff_lambda, diff_subln, w_br, w_o, ffn2_norm, ffn2_wi, ffn2_wo, ple_norm, ple_gate_w, ple_proj_w):
    b, s, d = x.shape
    depth = p.shape[0]
    t = b * s
    tm = min(512, t)
    tq = min(ATTN_BLOCK, s)
    n_chunks = D_FF // FF_CHUNK

    idx = jnp.arange(LANES)
    grp = jnp.where((idx[:, None] // HEAD_DIM) == (idx[None, :] // HEAD_DIM), 1.0 / HEAD_DIM, 0.0).astype(BF16)
    tri_incl = (idx[:, None] >= idx[None, :]).astype(BF16)
    eye = jnp.eye(max(tq, LANES), dtype=BF16)
    later = (idx[None, :] > idx[:, None]).astype(BF16)
    lu = jnp.concatenate([jnp.concatenate([later, later], axis=1), jnp.ones((ONES_ROWS, 2 * LANES), BF16)], axis=0)

    rc, rs1, rs2 = _rope_tables(positions, tm)

    def ffn_weights(wi, wo):
        wa = wi[:, :D_FF].reshape(d, n_chunks, FF_CHUNK).transpose(1, 0, 2).astype(BF16)
        wg = wi[:, D_FF:].reshape(d, n_chunks, FF_CHUNK).transpose(1, 0, 2).astype(BF16)
        return wa, wg, wo.reshape(n_chunks, FF_CHUNK, d).astype(BF16)

    fb_head = jnp.arange(FB_COLS) // LANES
    h = x.reshape(t, d)
    for i in range(depth):
        lam_init = 0.8 - 0.6 * math.exp(-0.3 * i)
        o = IN_OFFSETS
        fb_w = w_in[i][:, o["fb"]:o["c"]][:, fb_head]
        w_proj = jnp.concatenate([w_in[i][:, o["a"]:o["fb"]], w_in[i][:, o["c"]:o["gates"]], fb_w], axis=1).astype(BF16)
        w_gate = w_in[i][:, o["gates"]:o["end"]].astype(BF16)
        fb_bias = b_forget[i][fb_head][None]

        h = _ffn(h, ffn1_norm[i][None], *ffn_weights(ffn1_wi[i], ffn1_wo[i]), tm)
        ya, yb, yc, fb = _inproj(h, mix_norm[i][None], w_proj, _pair_tile(qk_gain_fox[i]),
                                 _pair_tile(qk_gain_diff[i]), grp, rc, rs1, rs2, tm)
        negcum = _forget_cumsum(fb.reshape(b, s, FB_COLS), fb_bias, tri_incl)
        oa = _attn_sb(ya.reshape(b, s, -1), lu, eye, tq, SB_PAIRS)
        ob = _attn_fox(yb.reshape(b, s, -1), negcum, eye, tq, FOX_TILES)
        oc = _attn_diff(yc.reshape(b, s, -1), diff_lambda[i], diff_subln[i][:, None], eye, tq, DIFF_TILES, lam_init)
        h = _merge(h, oa.reshape(t, -1), ob.reshape(t, -1), oc.reshape(t, -1), mix_norm[i][None],
                   w_gate, w_br[i].astype(BF16), w_o[i].astype(BF16), tm)
        h = _ffn(h, ffn2_norm[i][None], *ffn_weights(ffn2_wi[i], ffn2_wo[i]), tm)
        h = _ple(h, p[i].reshape(t, P_DIM), ple_norm[i][None], ple_gate_w[i].astype(BF16),
                 ple_proj_w[i].astype(BF16), tm)
    return h.reshape(b, s, d)
```

```python
import functools
import math

import jax
import jax.numpy as jnp
from jax import lax
from jax.experimental import pallas as pl
from jax.experimental.pallas import tpu as pltpu

D_MODEL = 1024
HEAD_DIM = 64
LANES = 128
N_PAIR_SB = 2
N_PAIR_FOX = 2
H_DIFF = 4
W_SB = 256
W_FOX = 256
W_DIFF = 512
ROT_DIM = 16
ROPE_THETA = 500000.0
D_FF = 2816
FF_CHUNK = 256
ATTN_BLOCK = 256
SB_PAIRS = 2
FOX_TILES = 2
DIFF_TILES = 2
P_DIM = 256
EPS = 1e-6
CHUNK = 64
LOG2E = math.log2(math.e)
SCALE = HEAD_DIM ** -0.5 * LOG2E
FB_COLS = 2 * N_PAIR_FOX * LANES
IN_OFFSETS = {"a": 0, "b": 768, "fb": 1536, "c": 1540, "gates": 3076, "end": 6148}
VMEM_LIMIT = 56 * 1024 * 1024

F32 = jnp.float32
BF16 = jnp.bfloat16
NT_DIMS = (((1,), (1,)), ((), ()))


def _dot(a, b):
    return jnp.dot(a, b, preferred_element_type=F32)


def _dot_nt(a, b):
    return lax.dot_general(a, b, NT_DIMS, preferred_element_type=F32)


def _rms(x, gain):
    ms = jnp.mean(x * x, axis=-1, keepdims=True)
    return x * lax.rsqrt(ms + EPS) * gain


def _split_bf16(x, parts):
    out = []
    for _ in range(parts - 1):
        bits = lax.bitcast_convert_type(x, jnp.uint32) & jnp.uint32(0xFFFF0000)
        piece = lax.bitcast_convert_type(bits, F32)
        out.append(piece.astype(BF16))
        x = x - piece
    out.append(x.astype(BF16))
    return out


def _neg_abs(x):
    bits = lax.bitcast_convert_type(x, jnp.uint32) | jnp.uint32(0x80000000)
    return lax.bitcast_convert_type(bits, F32)


def _params(n_grid):
    return pltpu.CompilerParams(
        dimension_semantics=("arbitrary",) * n_grid, vmem_limit_bytes=VMEM_LIMIT)


def _const_spec(shape):
    zeros = (0,) * len(shape)
    return pl.BlockSpec(shape, lambda *_: zeros, pipeline_mode=pl.Buffered(1))


def _ffn_kernel(h_ref, gain_ref, wa_ref, wg_ref, wo_ref, o_ref, acc_ref):
    h = h_ref[...]
    xn = _rms(h, gain_ref[...]).astype(BF16)
    acc_ref[...] = jnp.zeros_like(acc_ref)

    def chunk(c, carry):
        a = _dot(xn, wa_ref[c])
        g = _dot(xn, wg_ref[c])
        act = (a * jax.nn.sigmoid(a) * g).astype(BF16)
        acc_ref[...] += _dot(act, wo_ref[c])
        return carry

    lax.fori_loop(0, wa_ref.shape[0], chunk, 0, unroll=True)
    o_ref[...] = h + 0.5 * acc_ref[...]


def _ffn(h, gain, wa, wg, wo, tm):
    t = h.shape[0]
    row = pl.BlockSpec((tm, D_MODEL), lambda i: (i, 0))
    return pl.pallas_call(
        _ffn_kernel,
        grid=(t // tm,),
        in_specs=[row, _const_spec(gain.shape), _const_spec(wa.shape), _const_spec(wg.shape),
                  _const_spec(wo.shape)],
        out_specs=row,
        out_shape=jax.ShapeDtypeStruct(h.shape, F32),
        scratch_shapes=[pltpu.VMEM((tm, D_MODEL), F32)],
        compiler_params=_params(1),
        name="ffn",
    )(h, gain, wa, wg, wo)


def _rope_kernel(pos_ref, const_ref, c_ref, s1_ref, s2_ref):
    ang = pos_ref[...].astype(F32) * const_ref[0:1, :]
    sin = jnp.sin(ang)
    c_ref[...] = jnp.cos(ang)
    s1_ref[...] = sin * const_ref[1:2, :]
    s2_ref[...] = sin * const_ref[2:3, :]


def _rope_tables(positions, tm):
    t = positions.size
    lane = jnp.arange(LANES) % HEAD_DIM
    inv_freq = ROPE_THETA ** (-jnp.arange(0, ROT_DIM, 2, dtype=F32) / ROT_DIM)
    half = ROT_DIM // 2
    freq_lane = jnp.where(lane < ROT_DIM, inv_freq[lane % half], 0.0)
    plus = jnp.where((lane >= half) & (lane < ROT_DIM), 1.0, 0.0)
    minus = jnp.where(lane < half, -1.0, 0.0)
    const = jnp.zeros((8, LANES), F32).at[0].set(freq_lane).at[1].set(plus).at[2].set(minus)
    out = jax.ShapeDtypeStruct((t, LANES), F32)
    row = pl.BlockSpec((tm, LANES), lambda i: (i, 0))
    return pl.pallas_call(
        _rope_kernel,
        grid=(t // tm,),
        in_specs=[pl.BlockSpec((tm, 1), lambda i: (i, 0)), _const_spec(const.shape)],
        out_specs=[row, row, row],
        out_shape=[out, out, out],
        compiler_params=_params(1),
        name="rope_tables",
    )(positions.reshape(t, 1), const)


def _inproj_kernel(h_ref, gain_ref, w_ref, gfox_ref, gdiff_ref, grp_ref, rc_ref, rs1_ref, rs2_ref,
                   ya_ref, yb_ref, yc_ref, fb_ref):
    xn = _rms(h_ref[...], gain_ref[...]).astype(BF16)
    grp = grp_ref[...]

    def head_norm(x, gain):
        ms = _dot((x * x).astype(BF16), grp)
        return x * lax.rsqrt(ms + EPS) * gain

    def rope(x):
        return (x * rc_ref[...] + pltpu.roll(x, ROT_DIM // 2, 1) * rs1_ref[...]
                + pltpu.roll(x, LANES - ROT_DIM // 2, 1) * rs2_ref[...])

    y = _dot(xn, w_ref[:, 0:768])
    ya_ref[:, 0:W_SB] = (y[:, 0:W_SB] * SCALE).astype(BF16)
    ya_ref[:, W_SB:768] = y[:, W_SB:768].astype(BF16)

    y = _dot(xn, w_ref[:, 768:1536])
    for blk in range(2 * N_PAIR_FOX):
        is_q = blk < N_PAIR_FOX
        x = head_norm(y[:, blk * LANES:(blk + 1) * LANES], gfox_ref[(0 if is_q else 1):(1 if is_q else 2), :])
        if is_q:
            x = x * SCALE
        yb_ref[:, blk * LANES:(blk + 1) * LANES] = x.astype(BF16)
    yb_ref[:, 2 * W_FOX:768] = y[:, 2 * W_FOX:768].astype(BF16)

    y = _dot(xn, w_ref[:, 1536:3072])
    for blk in range(2 * H_DIFF):
        is_q = blk < H_DIFF
        x = head_norm(y[:, blk * LANES:(blk + 1) * LANES], gdiff_ref[(0 if is_q else 1):(1 if is_q else 2), :])
        x = rope(x)
        if is_q:
            x = x * SCALE
        yc_ref[:, blk * LANES:(blk + 1) * LANES] = x.astype(BF16)
    yc_ref[:, 2 * W_DIFF:1536] = y[:, 2 * W_DIFF:1536].astype(BF16)

    fb_ref[...] = _dot(xn, w_ref[:, 3072:3072 + FB_COLS])


def _inproj(h, gain, w, gfox, gdiff, grp, rc, rs1, rs2, tm):
    t = h.shape[0]
    row = lambda n: pl.BlockSpec((tm, n), lambda i: (i, 0))
    return pl.pallas_call(
        _inproj_kernel,
        grid=(t // tm,),
        in_specs=[row(D_MODEL), _const_spec(gain.shape), _const_spec(w.shape), _const_spec(gfox.shape),
                  _const_spec(gdiff.shape), _const_spec(grp.shape), row(LANES), row(LANES), row(LANES)],
        out_specs=[row(768), row(768), row(1536), row(FB_COLS)],
        out_shape=[jax.ShapeDtypeStruct((t, 768), BF16), jax.ShapeDtypeStruct((t, 768), BF16),
                   jax.ShapeDtypeStruct((t, 1536), BF16), jax.ShapeDtypeStruct((t, FB_COLS), F32)],
        compiler_params=_params(1),
        name="inproj",
    )(h, gain, w, gfox, gdiff, grp, rc, rs1, rs2)


def _forget_kernel(fb_ref, bias_ref, tri_ref, o_ref):
    s = fb_ref.shape[1]
    tri = tri_ref[...]
    carry = jnp.zeros((1, fb_ref.shape[2]), F32)
    for blk in range(s // LANES):
        x = fb_ref[0, blk * LANES:(blk + 1) * LANES, :] + bias_ref[...]
        log_f = jnp.minimum(x, 0.0) - jnp.log(1.0 + jnp.exp(-jnp.abs(x)))
        cum = carry
        for piece in _split_bf16(log_f, 3):
            cum = cum + _dot(tri, piece)
        o_ref[0, blk * LANES:(blk + 1) * LANES, :] = cum * -LOG2E
        carry = cum[LANES - 1:LANES, :]


def _forget_cumsum(fb, bias, tri):
    b, s, cols = fb.shape
    spec = pl.BlockSpec((1, s, cols), lambda i: (i, 0, 0))
    return pl.pallas_call(
        _forget_kernel,
        grid=(b,),
        in_specs=[spec, _const_spec(bias.shape), _const_spec(tri.shape)],
        out_specs=spec,
        out_shape=jax.ShapeDtypeStruct(fb.shape, F32),
        compiler_params=_params(1),
        name="forget_cumsum",
    )(fb, bias, tri)


def _merge_kernel(h_ref, oa_ref, ob_ref, oc_ref, gain_ref, wgate_ref, wbr_ref, wo_ref, o_ref):
    h = h_ref[...]
    u = _rms(h, gain_ref[...]).astype(BF16)
    merged = None
    row0 = 0
    for br, o_br in enumerate((oa_ref, ob_ref, oc_ref)):
        width = o_br.shape[1]
        y = _dot(o_br[...], wbr_ref[row0:row0 + width, :])
        gate = jax.nn.sigmoid(_dot(u, wgate_ref[:, br * D_MODEL:(br + 1) * D_MODEL]))
        merged = gate * y if merged is None else merged + gate * y
        row0 += width
    o_ref[...] = h + _dot(merged.astype(BF16), wo_ref[...])


def _merge(h, oa, ob, oc, gain, wgate, wbr, wo, tm):
    t = h.shape[0]
    row = lambda n: pl.BlockSpec((tm, n), lambda i: (i, 0))
    return pl.pallas_call(
        _merge_kernel,
        grid=(t // tm,),
        in_specs=[row(D_MODEL), row(W_SB), row(W_FOX), row(W_DIFF), _const_spec(gain.shape),
                  _const_spec(wgate.shape), _const_spec(wbr.shape), _const_spec(wo.shape)],
        out_specs=row(D_MODEL),
        out_shape=jax.ShapeDtypeStruct(h.shape, F32),
        compiler_params=_params(1),
        name="merge",
    )(h, oa, ob, oc, gain, wgate, wbr, wo)


def _ple_kernel(h_ref, p_ref, gain_ref, wgate_ref, wproj_ref, o_ref):
    h = h_ref[...]
    u = _rms(h, gain_ref[...]).astype(BF16)
    gate = jax.nn.sigmoid(_dot(u, wgate_ref[...]))
    o_ref[...] = h + gate * _dot(p_ref[...].astype(BF16), wproj_ref[...])


def _ple(h, p, gain, wgate, wproj, tm):
    t = h.shape[0]
    row = lambda n: pl.BlockSpec((tm, n), lambda i: (i, 0))
    return pl.pallas_call(
        _ple_kernel,
        grid=(t // tm,),
        in_specs=[row(D_MODEL), row(P_DIM), _const_spec(gain.shape), _const_spec(wgate.shape),
                  _const_spec(wproj.shape)],
        out_specs=row(D_MODEL),
        out_shape=jax.ShapeDtypeStruct(h.shape, F32),
        compiler_params=_params(1),
        name="ple",
    )(h, p, gain, wgate, wproj)


TRANSPOSE_ROWS = 512
NO_BLOCK_LOGIT = -1e30


def _low_lanes():
    return lax.broadcasted_iota(jnp.int32, (1, LANES), 1) < HEAD_DIM


def _block_iota(tq, axis):
    return lax.broadcasted_iota(jnp.int32, (tq, tq), axis)


def _block_ds(kb, tq):
    if isinstance(kb, int):
        return pl.ds(kb * tq, tq)
    return pl.ds(pl.multiple_of(kb * tq, tq), tq)


def _mxu_transpose(x, eye_ref):
    n = x.shape[1]
    return _dot_nt(eye_ref[0:n, 0:n], x).astype(BF16)


def _for_each_transposed_chunk(src_ref, lanes, eye_ref, write):
    s = src_ref.shape[1]
    step = min(TRANSPOSE_ROWS, s)
    for start in range(0, s, step):
        pos = slice(start, start + step)
        write(pos, _mxu_transpose(src_ref[0, pos, lanes], eye_ref))


def _split_heads(q):
    low = _low_lanes()
    zero = jnp.zeros_like(q)
    return jnp.where(low, q, zero), jnp.where(low, zero, q)


def _attn_sb_kernel(q_ref, k_ref, v_ref, lu_ref, eye_ref, o_ref, vt_ref, lb_ref, lk_ref, a_ref, acc_ref, run_ref,
                    *, tq):
    i = pl.program_id(2)
    pairs = q_ref.shape[2] // LANES

    def tile(c):
        return slice(c * LANES, (c + 1) * LANES)

    @pl.when(i == 0)
    def _():
        for c in range(pairs):
            def write(pos, vt, c=c):
                vt_ref[tile(c), pos] = vt

            _for_each_transposed_chunk(v_ref, tile(c), eye_ref, write)

    qs = [qh for c in range(pairs) for qh in _split_heads(q_ref[0, :, tile(c)])]
    strict = _block_iota(tq, 0) < _block_iota(tq, 1)
    lu = lu_ref[...]
    heads = range(2 * pairs)
    n_tiles = tq // LANES

    def logits(kb):
        return [_dot_nt(k_ref[0, _block_ds(kb, tq), tile(h // 2)], qs[h]) for h in heads]

    def logits_stage(zs, slot, diag):
        for h in heads:
            z = zs[h]
            log_beta = jnp.minimum(z, 0.0) - jnp.log2(1.0 + jnp.exp2(_neg_abs(z)))
            log_keep = log_beta - z
            if diag:
                log_keep = jnp.where(strict, log_keep, 0.0)
                log_beta = jnp.where(strict, log_beta, -jnp.inf)
            lb_ref[slot, h] = log_beta
            hi, lo = _split_bf16(log_keep, 2)
            for c in range(n_tiles):
                lk_ref[slot, h, 2 * c * LANES:(2 * c + 1) * LANES, :] = hi[c * LANES:(c + 1) * LANES, :]
                lk_ref[slot, h, (2 * c + 1) * LANES:(2 * c + 2) * LANES, :] = lo[c * LANES:(c + 1) * LANES, :]

    def suffix_sums(slot):
        return [[_dot(lu, lk_ref[slot, h, 2 * c * LANES:(2 * c + 2) * LANES, :]) for c in range(n_tiles)]
                for h in heads]

    def weights_stage(slot, h, sums, run):
        for c in reversed(range(n_tiles)):
            sl = slice(c * LANES, (c + 1) * LANES)
            a_ref[slot, h, sl, :] = jnp.exp2(lb_ref[slot, h, sl, :] + sums[c][0:LANES, :] + run).astype(BF16)
            run = run + sums[c][LANES:LANES + 1, :]
        return run

    def products(slot, kb):
        return [_dot(vt_ref[h * HEAD_DIM:(h + 1) * HEAD_DIM, _block_ds(kb, tq)], a_ref[slot, h]) for h in heads]

    def accumulate(pv):
        for h in heads:
            acc_ref[h] += pv[h]

    def step(n, cur):
        kb = i - n
        zs = logits(jnp.maximum(kb - 1, 0))
        if cur == 0:
            zs = [jnp.where(kb >= 1, z, NO_BLOCK_LOGIT) for z in zs]
        sums = suffix_sums(cur)
        pv = products(1 - cur, jnp.minimum(kb + 1, i))
        for h in heads:
            run_ref[h] = weights_stage(cur, h, sums[h], run_ref[h])
        logits_stage(zs, 1 - cur, False)
        accumulate(pv)

    def two_steps(t, carry):
        step(2 * t, 0)
        step(2 * t + 1, 1)
        return carry

    logits_stage(logits(i), 0, True)
    a_ref[1] = jnp.zeros(a_ref.shape[1:], BF16)
    acc_ref[...] = jnp.zeros_like(acc_ref)
    run_ref[...] = jnp.zeros_like(run_ref)
    lax.fori_loop(0, i // 2 + 1, two_steps, 0)
    accumulate(products(1, 0))
    for c in range(pairs):
        o_t = jnp.concatenate([acc_ref[2 * c], acc_ref[2 * c + 1]], axis=0).astype(BF16)
        o_ref[0, :, tile(c)] = _mxu_transpose(o_t, eye_ref)


def _attn_sb(ya, lu, eye, tq, pairs):
    b, s, _ = ya.shape
    groups = N_PAIR_SB // pairs
    width = pairs * LANES
    heads = 2 * pairs
    return pl.pallas_call(
        functools.partial(_attn_sb_kernel, tq=tq),
        grid=(b, groups, s // tq),
        in_specs=[pl.BlockSpec((1, tq, width), lambda bi, j, i: (bi, i, j)),
                  pl.BlockSpec((1, s, width), lambda bi, j, i: (bi, 0, groups + j)),
                  pl.BlockSpec((1, s, width), lambda bi, j, i: (bi, 0, 2 * groups + j)),
                  _const_spec(lu.shape), _const_spec(eye.shape)],
        out_specs=pl.BlockSpec((1, tq, width), lambda bi, j, i: (bi, i, j)),
        out_shape=jax.ShapeDtypeStruct((b, s, W_SB), BF16),
        scratch_shapes=[pltpu.VMEM((width, s), BF16),
                        pltpu.VMEM((2, heads, tq, tq), F32), pltpu.VMEM((2, heads, 2 * tq, tq), BF16),
                        pltpu.VMEM((2, heads, tq, tq), BF16), pltpu.VMEM((heads, HEAD_DIM, tq), F32),
                        pltpu.VMEM((heads, 1, tq), F32)],
        compiler_params=_params(3),
        name="attn_stickbreak",
    )(ya, ya, ya, lu, eye)


def _softmax_scratch(n_streams, tq, acc_rows):
    return [pltpu.VMEM((2, n_streams, tq, tq), F32), pltpu.VMEM((2, n_streams, tq, tq), BF16),
            pltpu.VMEM((n_streams, acc_rows, tq), F32), pltpu.VMEM((2, n_streams, 1, tq), F32)]


def _softmax_walk(i, diag_scores, scores, values, s_ref, p_ref, acc_ref, stat_ref, tq):
    n_st = len(diag_scores)
    streams = range(n_st)
    q_tiles = range(tq // LANES)

    def products(slot, kb):
        v = values(kb)
        return [_dot(v[st], p_ref[slot, st]) for st in streams]

    def accumulate(alpha, pv):
        for st in streams:
            acc_ref[st] = acc_ref[st] * alpha[st] + pv[st]

    def step(n, cur):
        kb = i - n
        s_next = scores(jnp.maximum(kb - 1, 0))
        for st in streams:
            s_ref[1 - cur, st] = jnp.where(kb >= 1, s_next[st], -jnp.inf) if cur == 0 else s_next[st]
        accumulate([stat_ref[1, st] for st in streams], products(1 - cur, jnp.minimum(kb + 1, i)))
        for st in streams:
            for j in q_tiles:
                cols = slice(j * LANES, (j + 1) * LANES)
                s = s_ref[cur, st, :, cols]
                m_old = stat_ref[0, st, :, cols]
                m_j = jnp.maximum(m_old, jnp.max(s, axis=0, keepdims=True))
                p_ref[cur, st, :, cols] = jnp.exp2(s - m_j).astype(BF16)
                stat_ref[0, st, :, cols] = m_j
                stat_ref[1, st, :, cols] = jnp.exp2(m_old - m_j)

    def two_steps(t, carry):
        step(2 * t, 0)
        step(2 * t + 1, 1)
        return carry

    for st in streams:
        s_ref[0, st] = diag_scores[st]
    p_ref[1] = jnp.zeros(p_ref.shape[1:], BF16)
    acc_ref[...] = jnp.zeros_like(acc_ref)
    stat_ref[0] = jnp.full(stat_ref.shape[1:], -jnp.inf, F32)
    stat_ref[1] = jnp.ones(stat_ref.shape[1:], F32)
    lax.fori_loop(0, i // 2 + 1, two_steps, 0)
    accumulate([stat_ref[1, st] for st in streams], products(1, 0))
    return [acc_ref[st] for st in streams]


ONES_ROWS = 16


def _attn_fox_kernel(q_ref, k_ref, v_ref, nc_ref, eye_ref, o_ref, vt_ref, s_ref, p_ref, acc_ref, stat_ref, *,
                     tq):
    i = pl.program_id(2)
    tiles = q_ref.shape[2] // LANES

    def tile(c):
        return slice(c * LANES, (c + 1) * LANES)

    @pl.when(i == 0)
    def _():
        for c in range(tiles):
            def write(pos, vt, c=c):
                for h in range(2):
                    vt_ref[2 * c + h, 0:HEAD_DIM, pos] = vt[h * HEAD_DIM:(h + 1) * HEAD_DIM]

            _for_each_transposed_chunk(v_ref, tile(c), eye_ref, write)
        vt_ref[:, HEAD_DIM:, :] = jnp.ones((2 * tiles, ONES_ROWS, vt_ref.shape[2]), BF16)

    qs = [qh for c in range(tiles) for qh in _split_heads(q_ref[0, :, tile(c)])]

    def scores(kb):
        out = []
        for st in range(2 * tiles):
            bias = nc_ref[0, _block_ds(kb, tq), tile(st)]
            s = _dot_nt(k_ref[0, _block_ds(kb, tq), tile(st // 2)], qs[st])
            out.append(s + jnp.concatenate([bias] * (tq // LANES), axis=1))
        return tuple(out)

    def values(kb):
        return tuple(vt_ref[st, :, _block_ds(kb, tq)] for st in range(2 * tiles))

    causal = _block_iota(tq, 0) <= _block_iota(tq, 1)
    diag = tuple(jnp.where(causal, s, -jnp.inf) for s in scores(i))
    acc = _softmax_walk(i, diag, scores, values, s_ref, p_ref, acc_ref, stat_ref, tq)
    outs = [a[0:HEAD_DIM] / a[HEAD_DIM:HEAD_DIM + 1] for a in acc]
    for c in range(tiles):
        o_t = jnp.concatenate(outs[2 * c:2 * c + 2], axis=0).astype(BF16)
        o_ref[0, :, tile(c)] = _mxu_transpose(o_t, eye_ref)


def _attn_fox(yb, negcum, eye, tq, tiles):
    b, s, _ = yb.shape
    groups = N_PAIR_FOX // tiles
    width = tiles * LANES
    return pl.pallas_call(
        functools.partial(_attn_fox_kernel, tq=tq),
        grid=(b, groups, s // tq),
        in_specs=[pl.BlockSpec((1, tq, width), lambda bi, j, i: (bi, i, j)),
                  pl.BlockSpec((1, s, width), lambda bi, j, i: (bi, 0, groups + j)),
                  pl.BlockSpec((1, s, width), lambda bi, j, i: (bi, 0, 2 * groups + j)),
                  pl.BlockSpec((1, s, 2 * width), lambda bi, j, i: (bi, 0, j)), _const_spec(eye.shape)],
        out_specs=pl.BlockSpec((1, tq, width), lambda bi, j, i: (bi, i, j)),
        out_shape=jax.ShapeDtypeStruct((b, s, W_FOX), BF16),
        scratch_shapes=[pltpu.VMEM((2 * tiles, HEAD_DIM + ONES_ROWS, s), BF16)]
        + _softmax_scratch(2 * tiles, tq, HEAD_DIM + ONES_ROWS),
        compiler_params=_params(3),
        name="attn_forget",
    )(yb, yb, yb, negcum, eye)


def _attn_diff_kernel(q_ref, k_ref, v_ref, lam_ref, subln_ref, eye_ref, o_ref, vt_ref, s_ref, p_ref, acc_ref,
                      stat_ref, *, tq, lam_init):
    i = pl.program_id(2)
    tiles = q_ref.shape[2] // LANES

    def tile(c):
        return slice(c * LANES, (c + 1) * LANES)

    @pl.when(i == 0)
    def _():
        for c in range(tiles):
            def write(pos, vt, c=c):
                vt_ref[c, 0:LANES, pos] = vt

            _for_each_transposed_chunk(v_ref, tile(c), eye_ref, write)
        vt_ref[:, LANES:, :] = jnp.ones((tiles, ONES_ROWS, vt_ref.shape[2]), BF16)

    qs = [qh for c in range(tiles) for qh in _split_heads(q_ref[0, :, tile(c)])]

    def scores(kb):
        out = []
        for c in range(tiles):
            k = k_ref[0, _block_ds(kb, tq), tile(c)]
            out += [_dot_nt(k, qs[2 * c + h]) for h in range(2)]
        return tuple(out)

    def values(kb):
        out = []
        for c in range(tiles):
            vt = vt_ref[c, :, _block_ds(kb, tq)]
            out += [vt, vt]
        return tuple(out)

    visible = (_block_iota(tq, 0) // CHUNK) <= (_block_iota(tq, 1) // CHUNK)
    diag = tuple(jnp.where(visible, s, -jnp.inf) for s in scores(i))
    acc = _softmax_walk(i, diag, scores, values, s_ref, p_ref, acc_ref, stat_ref, tq)

    lf = lam_ref[...]
    lam = (jnp.exp(jnp.sum(lf[0:1] * lf[1:2], axis=1, keepdims=True))
           - jnp.exp(jnp.sum(lf[2:3] * lf[3:4], axis=1, keepdims=True)) + lam_init)
    for c in range(tiles):
        a1, a2 = (a[0:LANES] / a[LANES:LANES + 1] for a in acc[2 * c:2 * c + 2])
        o_t = a1 - lam * a2
        ms = jnp.mean(o_t * o_t, axis=0, keepdims=True)
        o_t = o_t * lax.rsqrt(ms + EPS) * subln_ref[...] * (1.0 - lam_init)
        o_ref[0, :, tile(c)] = _mxu_transpose(o_t.astype(BF16), eye_ref)


def _attn_diff(yc, diff_lambda, subln, eye, tq, tiles, lam_init):
    b, s, _ = yc.shape
    groups = H_DIFF // tiles
    width = tiles * LANES
    return pl.pallas_call(
        functools.partial(_attn_diff_kernel, tq=tq, lam_init=lam_init),
        grid=(b, groups, s // tq),
        in_specs=[pl.BlockSpec((1, tq, width), lambda bi, j, i: (bi, i, j)),
                  pl.BlockSpec((1, s, width), lambda bi, j, i: (bi, 0, groups + j)),
                  pl.BlockSpec((1, s, width), lambda bi, j, i: (bi, 0, 2 * groups + j)),
                  _const_spec(diff_lambda.shape), _const_spec(subln.shape), _const_spec(eye.shape)],
        out_specs=pl.BlockSpec((1, tq, width), lambda bi, j, i: (bi, i, j)),
        out_shape=jax.ShapeDtypeStruct((b, s, W_DIFF), BF16),
        scratch_shapes=[pltpu.VMEM((tiles, LANES + ONES_ROWS, s), BF16)]
        + _softmax_scratch(2 * tiles, tq, LANES + ONES_ROWS),
        compiler_params=_params(3),
        name="attn_diff",
    )(yc, yc, yc, diff_lambda, subln, eye)


def _pair_tile(x):
    return jnp.concatenate([x, x], axis=-1)


def kernel(x, p, positions, ffn1_norm, ffn1_wi, ffn1_wo, mix_norm, w_in, b_forget, qk_gain_fox, qk_gain_diff, diff_lambda, diff_subln, w_br, w_o, ffn2_norm, ffn2_wi, ffn2_wo, ple_norm, ple_gate_w, ple_proj_w):
    b, s, d = x.shape
    depth = p.shape[0]
    t = b * s
    tm = min(512, t)
    tq = min(ATTN_BLOCK, s)
    n_chunks = D_FF // FF_CHUNK

    idx = jnp.arange(LANES)
    grp = jnp.where((idx[:, None] // HEAD_DIM) == (idx[None, :] // HEAD_DIM), 1.0 / HEAD_DIM, 0.0).astype(BF16)
    tri_incl = (idx[:, None] >= idx[None, :]).astype(BF16)
    eye = jnp.eye(max(tq, LANES), dtype=BF16)
    later = (idx[None, :] > idx[:, None]).astype(BF16)
    lu = jnp.concatenate([jnp.concatenate([later, later], axis=1), jnp.ones((ONES_ROWS, 2 * LANES), BF16)], axis=0)

    rc, rs1, rs2 = _rope_tables(positions, tm)

    def ffn_weights(wi, wo):
        wa = wi[:, :D_FF].reshape(d, n_chunks, FF_CHUNK).transpose(1, 0, 2).astype(BF16)
        wg = wi[:, D_FF:].reshape(d, n_chunks, FF_CHUNK).transpose(1, 0, 2).astype(BF16)
        return wa, wg, wo.reshape(n_chunks, FF_CHUNK, d).astype(BF16)

    fb_head = jnp.arange(FB_COLS) // LANES
    h = x.reshape(t, d)
    for i in range(depth):
        lam_init = 0.8 - 0.6 * math.exp(-0.3 * i)
        o = IN_OFFSETS
        fb_w = w_in[i][:, o["fb"]:o["c"]][:, fb_head]
        w_proj = jnp.concatenate([w_in[i][:, o["a"]:o["fb"]], w_in[i][:, o["c"]:o["gates"]], fb_w], axis=1).astype(BF16)
        w_gate = w_in[i][:, o["gates"]:o["end"]].astype(BF16)
        fb_bias = b_forget[i][fb_head][None]

        h = _ffn(h, ffn1_norm[i][None], *ffn_weights(ffn1_wi[i], ffn1_wo[i]), tm)
        ya, yb, yc, fb = _inproj(h, mix_norm[i][None], w_proj, _pair_tile(qk_gain_fox[i]),
                                 _pair_tile(qk_gain_diff[i]), grp, rc, rs1, rs2, tm)
        negcum = _forget_cumsum(fb.reshape(b, s, FB_COLS), fb_bias, tri_incl)
        oa = _attn_sb(ya.reshape(b, s, -1), lu, eye, tq, SB_PAIRS)
        ob = _attn_fox(yb.reshape(b, s, -1), negcum, eye, tq, FOX_TILES)
        oc = _attn_diff(yc.reshape(b, s, -1), diff_lambda[i], diff_subln[i][:, None], eye, tq, DIFF_TILES, lam_init)
        h = _merge(h, oa.reshape(t, -1), ob.reshape(t, -1), oc.reshape(t, -1), mix_norm[i][None],
                   w_gate, w_br[i].astype(BF16), w_o[i].astype(BF16), tm)
        h = _ffn(h, ffn2_norm[i][None], *ffn_weights(ffn2_wi[i], ffn2_wo[i]), tm)
        h = _ple(h, p[i].reshape(t, P_DIM), ple_norm[i][None], ple_gate_w[i].astype(BF16),
                 ple_proj_w[i].astype(BF16), tm)
    return h.reshape(b, s, d)
```

```python
import functools
import math

import jax
import jax.numpy as jnp
from jax import lax
from jax.experimental import pallas as pl
from jax.experimental.pallas import tpu as pltpu

D_MODEL = 1024
HEAD_DIM = 64
LANES = 128
N_PAIR_SB = 2
N_PAIR_FOX = 2
H_DIFF = 4
W_SB = 256
W_FOX = 256
W_DIFF = 512
ROT_DIM = 16
ROPE_THETA = 500000.0
D_FF = 2816
FF_CHUNK = 256
ATTN_BLOCK = 256
SB_PAIRS = 2
FOX_TILES = 2
DIFF_TILES = 2
P_DIM = 256
EPS = 1e-6
CHUNK = 64
LOG2E = math.log2(math.e)
SCALE = HEAD_DIM ** -0.5 * LOG2E
FB_COLS = 2 * N_PAIR_FOX * LANES
IN_OFFSETS = {"a": 0, "b": 768, "fb": 1536, "c": 1540, "gates": 3076, "end": 6148}
VMEM_LIMIT = 56 * 1024 * 1024

F32 = jnp.float32
BF16 = jnp.bfloat16
NT_DIMS = (((1,), (1,)), ((), ()))


def _dot(a, b):
    return jnp.dot(a, b, preferred_element_type=F32)


def _dot_nt(a, b):
    return lax.dot_general(a, b, NT_DIMS, preferred_element_type=F32)


def _rms(x, gain):
    ms = jnp.mean(x * x, axis=-1, keepdims=True)
    return x * lax.rsqrt(ms + EPS) * gain


def _split_bf16(x, parts):
    out = []
    for _ in range(parts - 1):
        bits = lax.bitcast_convert_type(x, jnp.uint32) & jnp.uint32(0xFFFF0000)
        piece = lax.bitcast_convert_type(bits, F32)
        out.append(piece.astype(BF16))
        x = x - piece
    out.append(x.astype(BF16))
    return out


def _neg_abs(x):
    bits = lax.bitcast_convert_type(x, jnp.uint32) | jnp.uint32(0x80000000)
    return lax.bitcast_convert_type(bits, F32)


def _params(n_grid):
    return pltpu.CompilerParams(
        dimension_semantics=("arbitrary",) * n_grid, vmem_limit_bytes=VMEM_LIMIT)


def _const_spec(shape):
    zeros = (0,) * len(shape)
    return pl.BlockSpec(shape, lambda *_: zeros, pipeline_mode=pl.Buffered(1))


def _ffn_kernel(h_ref, gain_ref, wi_ref, wo_ref, o_ref, acc_ref):
    h = h_ref[...]
    xn = _rms(h, gain_ref[...]).astype(BF16)
    acc_ref[...] = jnp.zeros_like(acc_ref)
    for c in range(D_FF // FF_CHUNK):
        cols = slice(c * FF_CHUNK, (c + 1) * FF_CHUNK)
        a = _dot(xn, wi_ref[:, cols])
        g = _dot(xn, wi_ref[:, D_FF + c * FF_CHUNK:D_FF + (c + 1) * FF_CHUNK])
        act = (a * jax.nn.sigmoid(a) * g).astype(BF16)
        acc_ref[...] += _dot(act, wo_ref[cols, :])
    o_ref[...] = h + 0.5 * acc_ref[...]


def _ffn(h, gain, wi, wo, tm):
    t = h.shape[0]
    row = pl.BlockSpec((tm, D_MODEL), lambda i: (i, 0))
    return pl.pallas_call(
        _ffn_kernel,
        grid=(t // tm,),
        in_specs=[row, _const_spec(gain.shape), _const_spec(wi.shape), _const_spec(wo.shape)],
        out_specs=row,
        out_shape=jax.ShapeDtypeStruct(h.shape, F32),
        scratch_shapes=[pltpu.VMEM((tm, D_MODEL), F32)],
        compiler_params=_params(1),
        name="ffn",
    )(h, gain, wi, wo)


def _split_w_in_kernel(w_ref, wp_ref, wg_ref):
    o = IN_OFFSETS
    n_fb = o["c"] - o["fb"]
    wp_ref[0, :, 0:o["fb"]] = w_ref[0, :, 0:o["fb"]].astype(BF16)
    wp_ref[0, :, o["fb"]:o["gates"] - n_fb] = w_ref[0, :, o["c"]:o["gates"]].astype(BF16)
    for head in range(n_fb):
        col = w_ref[0, :, o["fb"] + head:o["fb"] + head + 1]
        start = o["gates"] - n_fb + head * LANES
        wp_ref[0, :, start:start + LANES] = jnp.broadcast_to(col, (col.shape[0], LANES)).astype(BF16)
    wg_ref[0] = w_ref[0, :, o["gates"]:o["end"]].astype(BF16)


def _split_w_in(w_in, rows):
    depth, d, cols = w_in.shape
    o = IN_OFFSETS
    n_proj = o["gates"] - (o["c"] - o["fb"]) + FB_COLS
    n_gate = o["end"] - o["gates"]
    blk = lambda n: pl.BlockSpec((1, rows, n), lambda l, r: (l, r, 0))
    return pl.pallas_call(
        _split_w_in_kernel,
        grid=(depth, d // rows),
        in_specs=[blk(cols)],
        out_specs=[blk(n_proj), blk(n_gate)],
        out_shape=[jax.ShapeDtypeStruct((depth, d, n_proj), BF16), jax.ShapeDtypeStruct((depth, d, n_gate), BF16)],
        compiler_params=_params(2),
        name="split_w_in",
    )(w_in)


def _rope_kernel(pos_ref, const_ref, c_ref, s1_ref, s2_ref):
    ang = pos_ref[...].astype(F32) * const_ref[0:1, :]
    sin = jnp.sin(ang)
    c_ref[...] = jnp.cos(ang)
    s1_ref[...] = sin * const_ref[1:2, :]
    s2_ref[...] = sin * const_ref[2:3, :]


def _rope_tables(positions, tm):
    t = positions.size
    lane = jnp.arange(LANES) % HEAD_DIM
    inv_freq = ROPE_THETA ** (-jnp.arange(0, ROT_DIM, 2, dtype=F32) / ROT_DIM)
    half = ROT_DIM // 2
    freq_lane = jnp.where(lane < ROT_DIM, inv_freq[lane % half], 0.0)
    plus = jnp.where((lane >= half) & (lane < ROT_DIM), 1.0, 0.0)
    minus = jnp.where(lane < half, -1.0, 0.0)
    const = jnp.zeros((8, LANES), F32).at[0].set(freq_lane).at[1].set(plus).at[2].set(minus)
    out = jax.ShapeDtypeStruct((t, LANES), F32)
    row = pl.BlockSpec((tm, LANES), lambda i: (i, 0))
    return pl.pallas_call(
        _rope_kernel,
        grid=(t // tm,),
        in_specs=[pl.BlockSpec((tm, 1), lambda i: (i, 0)), _const_spec(const.shape)],
        out_specs=[row, row, row],
        out_shape=[out, out, out],
        compiler_params=_params(1),
        name="rope_tables",
    )(positions.reshape(t, 1), const)


def _inproj_kernel(h_ref, gain_ref, w_ref, gfox_ref, gdiff_ref, grp_ref, rc_ref, rs1_ref, rs2_ref,
                   ya_ref, yb_ref, yc_ref, fb_ref):
    xn = _rms(h_ref[...], gain_ref[...]).astype(BF16)
    grp = grp_ref[...]

    def head_norm(x, gain):
        ms = _dot((x * x).astype(BF16), grp)
        return x * lax.rsqrt(ms + EPS) * gain

    def rope(x):
        return (x * rc_ref[...] + pltpu.roll(x, ROT_DIM // 2, 1) * rs1_ref[...]
                + pltpu.roll(x, LANES - ROT_DIM // 2, 1) * rs2_ref[...])

    y = _dot(xn, w_ref[:, 0:768])
    ya_ref[:, 0:W_SB] = (y[:, 0:W_SB] * SCALE).astype(BF16)
    ya_ref[:, W_SB:768] = y[:, W_SB:768].astype(BF16)

    y = _dot(xn, w_ref[:, 768:1536])
    for blk in range(2 * N_PAIR_FOX):
        is_q = blk < N_PAIR_FOX
        x = head_norm(y[:, blk * LANES:(blk + 1) * LANES], gfox_ref[(0 if is_q else 1):(1 if is_q else 2), :])
        if is_q:
            x = x * SCALE
        yb_ref[:, blk * LANES:(blk + 1) * LANES] = x.astype(BF16)
    yb_ref[:, 2 * W_FOX:768] = y[:, 2 * W_FOX:768].astype(BF16)

    y = _dot(xn, w_ref[:, 1536:3072])
    for blk in range(2 * H_DIFF):
        is_q = blk < H_DIFF
        x = head_norm(y[:, blk * LANES:(blk + 1) * LANES], gdiff_ref[(0 if is_q else 1):(1 if is_q else 2), :])
        x = rope(x)
        if is_q:
            x = x * SCALE
        yc_ref[:, blk * LANES:(blk + 1) * LANES] = x.astype(BF16)
    yc_ref[:, 2 * W_DIFF:1536] = y[:, 2 * W_DIFF:1536].astype(BF16)

    fb_ref[...] = _dot(xn, w_ref[:, 3072:3072 + FB_COLS])


def _inproj(h, gain, w, gfox, gdiff, grp, rc, rs1, rs2, tm):
    t = h.shape[0]
    row = lambda n: pl.BlockSpec((tm, n), lambda i: (i, 0))
    return pl.pallas_call(
        _inproj_kernel,
        grid=(t // tm,),
        in_specs=[row(D_MODEL), _const_spec(gain.shape), _const_spec(w.shape), _const_spec(gfox.shape),
                  _const_spec(gdiff.shape), _const_spec(grp.shape), row(LANES), row(LANES), row(LANES)],
        out_specs=[row(768), row(768), row(1536), row(FB_COLS)],
        out_shape=[jax.ShapeDtypeStruct((t, 768), BF16), jax.ShapeDtypeStruct((t, 768), BF16),
                   jax.ShapeDtypeStruct((t, 1536), BF16), jax.ShapeDtypeStruct((t, FB_COLS), F32)],
        compiler_params=_params(1),
        name="inproj",
    )(h, gain, w, gfox, gdiff, grp, rc, rs1, rs2)


def _forget_kernel(fb_ref, bias_ref, tri_ref, o_ref):
    s = fb_ref.shape[0]
    tri = tri_ref[...]
    carry = jnp.zeros((1, fb_ref.shape[1]), F32)
    for blk in range(s // LANES):
        x = fb_ref[blk * LANES:(blk + 1) * LANES, :] + bias_ref[...]
        log_f = jnp.minimum(x, 0.0) - jnp.log(1.0 + jnp.exp(-jnp.abs(x)))
        cum = carry
        for piece in _split_bf16(log_f, 3):
            cum = cum + _dot(tri, piece)
        o_ref[blk * LANES:(blk + 1) * LANES, :] = cum * -LOG2E
        carry = cum[LANES - 1:LANES, :]


def _forget_cumsum(fb, bias, tri, s):
    t, cols = fb.shape
    spec = pl.BlockSpec((s, cols), lambda i: (i, 0))
    return pl.pallas_call(
        _forget_kernel,
        grid=(t // s,),
        in_specs=[spec, _const_spec(bias.shape), _const_spec(tri.shape)],
        out_specs=spec,
        out_shape=jax.ShapeDtypeStruct(fb.shape, F32),
        compiler_params=_params(1),
        name="forget_cumsum",
    )(fb, bias, tri)


def _merge_kernel(h_ref, oa_ref, ob_ref, oc_ref, gain_ref, wgate_ref, wbr_ref, wo_ref, o_ref):
    h = h_ref[...]
    u = _rms(h, gain_ref[...]).astype(BF16)
    merged = None
    row0 = 0
    for br, o_br in enumerate((oa_ref, ob_ref, oc_ref)):
        width = o_br.shape[1]
        y = _dot(o_br[...], wbr_ref[row0:row0 + width, :])
        gate = jax.nn.sigmoid(_dot(u, wgate_ref[:, br * D_MODEL:(br + 1) * D_MODEL]))
        merged = gate * y if merged is None else merged + gate * y
        row0 += width
    o_ref[...] = h + _dot(merged.astype(BF16), wo_ref[...])


def _merge(h, oa, ob, oc, gain, wgate, wbr, wo, tm):
    t = h.shape[0]
    row = lambda n: pl.BlockSpec((tm, n), lambda i: (i, 0))
    return pl.pallas_call(
        _merge_kernel,
        grid=(t // tm,),
        in_specs=[row(D_MODEL), row(W_SB), row(W_FOX), row(W_DIFF), _const_spec(gain.shape),
                  _const_spec(wgate.shape), _const_spec(wbr.shape), _const_spec(wo.shape)],
        out_specs=row(D_MODEL),
        out_shape=jax.ShapeDtypeStruct(h.shape, F32),
        compiler_params=_params(1),
        name="merge",
    )(h, oa, ob, oc, gain, wgate, wbr, wo)


def _ple_kernel(h_ref, p_ref, gain_ref, wgate_ref, wproj_ref, o_ref):
    h = h_ref[...]
    u = _rms(h, gain_ref[...]).astype(BF16)
    gate = jax.nn.sigmoid(_dot(u, wgate_ref[...]))
    o_ref[...] = h + gate * _dot(p_ref[...].astype(BF16), wproj_ref[...])


def _ple(h, p, gain, wgate, wproj, tm):
    t = h.shape[0]
    row = lambda n: pl.BlockSpec((tm, n), lambda i: (i, 0))
    return pl.pallas_call(
        _ple_kernel,
        grid=(t // tm,),
        in_specs=[row(D_MODEL), row(P_DIM), _const_spec(gain.shape), _const_spec(wgate.shape),
                  _const_spec(wproj.shape)],
        out_specs=row(D_MODEL),
        out_shape=jax.ShapeDtypeStruct(h.shape, F32),
        compiler_params=_params(1),
        name="ple",
    )(h, p, gain, wgate, wproj)


TRANSPOSE_ROWS = 512
NO_BLOCK_LOGIT = -1e30


def _low_lanes():
    return lax.broadcasted_iota(jnp.int32, (1, LANES), 1) < HEAD_DIM


def _block_iota(tq, axis):
    return lax.broadcasted_iota(jnp.int32, (tq, tq), axis)


def _block_ds(kb, tq):
    if isinstance(kb, int):
        return pl.ds(kb * tq, tq)
    return pl.ds(pl.multiple_of(kb * tq, tq), tq)


def _mxu_transpose(x, eye_ref):
    n = x.shape[1]
    return _dot_nt(eye_ref[0:n, 0:n], x).astype(BF16)


def _for_each_transposed_chunk(src_ref, lanes, eye_ref, write):
    s = src_ref.shape[1]
    step = min(TRANSPOSE_ROWS, s)
    for start in range(0, s, step):
        pos = slice(start, start + step)
        write(pos, _mxu_transpose(src_ref[0, pos, lanes], eye_ref))


def _split_heads(q):
    low = _low_lanes()
    zero = jnp.zeros_like(q)
    return jnp.where(low, q, zero), jnp.where(low, zero, q)


def _store_split_heads(q_ref, qs_ref):
    for c in range(q_ref.shape[2] // LANES):
        for h, qh in enumerate(_split_heads(q_ref[0, :, c * LANES:(c + 1) * LANES])):
            qs_ref[2 * c + h] = qh


def _attn_sb_kernel(q_ref, k_ref, v_ref, lu_ref, eye_ref, o_ref, vt_ref, qs_ref, lb_ref, lk_ref, a_ref, acc_ref,
                    run_ref, *, tq):
    i = pl.program_id(2)
    pairs = q_ref.shape[2] // LANES

    def tile(c):
        return slice(c * LANES, (c + 1) * LANES)

    @pl.when(i == 0)
    def _():
        for c in range(pairs):
            def write(pos, vt, c=c):
                vt_ref[tile(c), pos] = vt

            _for_each_transposed_chunk(v_ref, tile(c), eye_ref, write)

    _store_split_heads(q_ref, qs_ref)
    strict = _block_iota(tq, 0) < _block_iota(tq, 1)
    lu = lu_ref[...]
    heads = range(2 * pairs)
    n_tiles = tq // LANES

    def logits(kb):
        return [_dot_nt(k_ref[0, _block_ds(kb, tq), tile(h // 2)], qs_ref[h]) for h in heads]

    def logits_stage(zs, slot, diag):
        for h in heads:
            z = zs[h]
            log_beta = jnp.minimum(z, 0.0) - jnp.log2(1.0 + jnp.exp2(_neg_abs(z)))
            log_keep = log_beta - z
            if diag:
                log_keep = jnp.where(strict, log_keep, 0.0)
                log_beta = jnp.where(strict, log_beta, -jnp.inf)
            lb_ref[slot, h] = log_beta
            hi, lo = _split_bf16(log_keep, 2)
            for c in range(n_tiles):
                lk_ref[slot, h, 2 * c * LANES:(2 * c + 1) * LANES, :] = hi[c * LANES:(c + 1) * LANES, :]
                lk_ref[slot, h, (2 * c + 1) * LANES:(2 * c + 2) * LANES, :] = lo[c * LANES:(c + 1) * LANES, :]

    def suffix_sums(slot):
        return [[_dot(lu, lk_ref[slot, h, 2 * c * LANES:(2 * c + 2) * LANES, :]) for c in range(n_tiles)]
                for h in heads]

    def weights_stage(slot, h, sums, run):
        for c in reversed(range(n_tiles)):
            sl = slice(c * LANES, (c + 1) * LANES)
            a_ref[slot, h, sl, :] = jnp.exp2(lb_ref[slot, h, sl, :] + sums[c][0:LANES, :] + run).astype(BF16)
            run = run + sums[c][LANES:LANES + 1, :]
        return run

    def products(slot, kb):
        return [_dot(vt_ref[h * HEAD_DIM:(h + 1) * HEAD_DIM, _block_ds(kb, tq)], a_ref[slot, h]) for h in heads]

    def accumulate(pv):
        for h in heads:
            acc_ref[h] += pv[h]

    def step(n, cur):
        kb = i - n
        zs = logits(jnp.maximum(kb - 1, 0))
        if cur == 0:
            zs = [jnp.where(kb >= 1, z, NO_BLOCK_LOGIT) for z in zs]
        sums = suffix_sums(cur)
        pv = products(1 - cur, jnp.minimum(kb + 1, i))
        for h in heads:
            run_ref[h] = weights_stage(cur, h, sums[h], run_ref[h])
        logits_stage(zs, 1 - cur, False)
        accumulate(pv)

    def two_steps(t, carry):
        step(2 * t, 0)
        step(2 * t + 1, 1)
        return carry

    logits_stage(logits(i), 0, True)
    a_ref[1] = jnp.zeros(a_ref.shape[1:], BF16)
    acc_ref[...] = jnp.zeros_like(acc_ref)
    run_ref[...] = jnp.zeros_like(run_ref)
    lax.fori_loop(0, i // 2 + 1, two_steps, 0)
    accumulate(products(1, 0))
    for c in range(pairs):
        o_t = jnp.concatenate([acc_ref[2 * c], acc_ref[2 * c + 1]], axis=0).astype(BF16)
        o_ref[0, :, tile(c)] = _mxu_transpose(o_t, eye_ref)


def _attn_sb(ya, lu, eye, tq, pairs):
    b, s, _ = ya.shape
    groups = N_PAIR_SB // pairs
    width = pairs * LANES
    heads = 2 * pairs
    return pl.pallas_call(
        functools.partial(_attn_sb_kernel, tq=tq),
        grid=(b, groups, s // tq),
        in_specs=[pl.BlockSpec((1, tq, width), lambda bi, j, i: (bi, i, j)),
                  pl.BlockSpec((1, s, width), lambda bi, j, i: (bi, 0, groups + j)),
                  pl.BlockSpec((1, s, width), lambda bi, j, i: (bi, 0, 2 * groups + j)),
                  _const_spec(lu.shape), _const_spec(eye.shape)],
        out_specs=pl.BlockSpec((1, tq, width), lambda bi, j, i: (bi, i, j)),
        out_shape=jax.ShapeDtypeStruct((b, s, W_SB), BF16),
        scratch_shapes=[pltpu.VMEM((width, s), BF16), pltpu.VMEM((heads, tq, LANES), BF16),
                        pltpu.VMEM((2, heads, tq, tq), F32), pltpu.VMEM((2, heads, 2 * tq, tq), BF16),
                        pltpu.VMEM((2, heads, tq, tq), BF16), pltpu.VMEM((heads, HEAD_DIM, tq), F32),
                        pltpu.VMEM((heads, 1, tq), F32)],
        compiler_params=_params(3),
        name="attn_stickbreak",
    )(ya, ya, ya, lu, eye)


def _softmax_scratch(n_streams, tq, acc_rows):
    return [pltpu.VMEM((2, n_streams, tq, tq), F32), pltpu.VMEM((2, n_streams, tq, tq), BF16),
            pltpu.VMEM((n_streams, acc_rows, tq), F32), pltpu.VMEM((2, n_streams, 1, tq), F32)]


def _softmax_walk(i, n_st, visible, score, value, s_ref, p_ref, acc_ref, stat_ref, tq):
    streams = range(n_st)
    q_tiles = range(tq // LANES)

    def accumulate(slot, kb):
        for st in streams:
            acc_ref[st] = acc_ref[st] * stat_ref[1, st] + _dot(value(st, kb), p_ref[slot, st])

    def step(n, cur):
        kb = i - n
        for st in streams:
            s_next = score(st, jnp.maximum(kb - 1, 0))
            s_ref[1 - cur, st] = jnp.where(kb >= 1, s_next, -jnp.inf) if cur == 0 else s_next
        accumulate(1 - cur, jnp.minimum(kb + 1, i))
        for st in streams:
            for j in q_tiles:
                cols = slice(j * LANES, (j + 1) * LANES)
                s = s_ref[cur, st, :, cols]
                m_old = stat_ref[0, st, :, cols]
                m_j = jnp.maximum(m_old, jnp.max(s, axis=0, keepdims=True))
                p_ref[cur, st, :, cols] = jnp.exp2(s - m_j).astype(BF16)
                stat_ref[0, st, :, cols] = m_j
                stat_ref[1, st, :, cols] = jnp.exp2(m_old - m_j)

    def two_steps(t, carry):
        step(2 * t, 0)
        step(2 * t + 1, 1)
        return carry

    for st in streams:
        s_ref[0, st] = jnp.where(visible, score(st, i), -jnp.inf)
    p_ref[1] = jnp.zeros(p_ref.shape[1:], BF16)
    acc_ref[...] = jnp.zeros_like(acc_ref)
    stat_ref[0] = jnp.full(stat_ref.shape[1:], -jnp.inf, F32)
    stat_ref[1] = jnp.ones(stat_ref.shape[1:], F32)
    lax.fori_loop(0, i // 2 + 1, two_steps, 0)
    accumulate(1, 0)
    return [acc_ref[st] for st in streams]


ONES_ROWS = 16


def _attn_fox_kernel(q_ref, k_ref, v_ref, nc_ref, eye_ref, o_ref, vt_ref, qs_ref, s_ref, p_ref, acc_ref,
                     stat_ref, *, tq):
    i = pl.program_id(2)
    tiles = q_ref.shape[2] // LANES

    def tile(c):
        return slice(c * LANES, (c + 1) * LANES)

    @pl.when(i == 0)
    def _():
        for c in range(tiles):
            def write(pos, vt, c=c):
                for h in range(2):
                    vt_ref[2 * c + h, 0:HEAD_DIM, pos] = vt[h * HEAD_DIM:(h + 1) * HEAD_DIM]

            _for_each_transposed_chunk(v_ref, tile(c), eye_ref, write)
        vt_ref[:, HEAD_DIM:, :] = jnp.ones((2 * tiles, ONES_ROWS, vt_ref.shape[2]), BF16)

    _store_split_heads(q_ref, qs_ref)

    def score(st, kb):
        bias = nc_ref[_block_ds(kb, tq), tile(st)]
        s = _dot_nt(k_ref[0, _block_ds(kb, tq), tile(st // 2)], qs_ref[st])
        return s + jnp.concatenate([bias] * (tq // LANES), axis=1)

    def value(st, kb):
        return vt_ref[st, :, _block_ds(kb, tq)]

    causal = _block_iota(tq, 0) <= _block_iota(tq, 1)
    acc = _softmax_walk(i, 2 * tiles, causal, score, value, s_ref, p_ref, acc_ref, stat_ref, tq)
    outs = [a[0:HEAD_DIM] / a[HEAD_DIM:HEAD_DIM + 1] for a in acc]
    for c in range(tiles):
        o_t = jnp.concatenate(outs[2 * c:2 * c + 2], axis=0).astype(BF16)
        o_ref[0, :, tile(c)] = _mxu_transpose(o_t, eye_ref)


def _attn_fox(yb, negcum, eye, tq, tiles):
    b, s, _ = yb.shape
    groups = N_PAIR_FOX // tiles
    width = tiles * LANES
    return pl.pallas_call(
        functools.partial(_attn_fox_kernel, tq=tq),
        grid=(b, groups, s // tq),
        in_specs=[pl.BlockSpec((1, tq, width), lambda bi, j, i: (bi, i, j)),
                  pl.BlockSpec((1, s, width), lambda bi, j, i: (bi, 0, groups + j)),
                  pl.BlockSpec((1, s, width), lambda bi, j, i: (bi, 0, 2 * groups + j)),
                  pl.BlockSpec((s, 2 * width), lambda bi, j, i: (bi, j)), _const_spec(eye.shape)],
        out_specs=pl.BlockSpec((1, tq, width), lambda bi, j, i: (bi, i, j)),
        out_shape=jax.ShapeDtypeStruct((b, s, W_FOX), BF16),
        scratch_shapes=[pltpu.VMEM((2 * tiles, HEAD_DIM + ONES_ROWS, s), BF16),
                        pltpu.VMEM((2 * tiles, tq, LANES), BF16)]
        + _softmax_scratch(2 * tiles, tq, HEAD_DIM + ONES_ROWS),
        compiler_params=_params(3),
        name="attn_forget",
    )(yb, yb, yb, negcum, eye)


def _attn_diff_kernel(q_ref, k_ref, v_ref, lam_ref, subln_ref, eye_ref, o_ref, vt_ref, qs_ref, s_ref, p_ref,
                      acc_ref, stat_ref, *, tq, lam_init):
    i = pl.program_id(2)
    tiles = q_ref.shape[2] // LANES

    def tile(c):
        return slice(c * LANES, (c + 1) * LANES)

    @pl.when(i == 0)
    def _():
        for c in range(tiles):
            def write(pos, vt, c=c):
                vt_ref[c, 0:LANES, pos] = vt

            _for_each_transposed_chunk(v_ref, tile(c), eye_ref, write)
        vt_ref[:, LANES:, :] = jnp.ones((tiles, ONES_ROWS, vt_ref.shape[2]), BF16)

    _store_split_heads(q_ref, qs_ref)

    def score(st, kb):
        return _dot_nt(k_ref[0, _block_ds(kb, tq), tile(st // 2)], qs_ref[st])

    def value(st, kb):
        return vt_ref[st // 2, :, _block_ds(kb, tq)]

    visible = (_block_iota(tq, 0) // CHUNK) <= (_block_iota(tq, 1) // CHUNK)
    acc = _softmax_walk(i, 2 * tiles, visible, score, value, s_ref, p_ref, acc_ref, stat_ref, tq)

    lf = lam_ref[...]
    lam = (jnp.exp(jnp.sum(lf[0:1] * lf[1:2], axis=1, keepdims=True))
           - jnp.exp(jnp.sum(lf[2:3] * lf[3:4], axis=1, keepdims=True)) + lam_init)
    for c in range(tiles):
        a1, a2 = (a[0:LANES] / a[LANES:LANES + 1] for a in acc[2 * c:2 * c + 2])
        o_t = a1 - lam * a2
        ms = jnp.mean(o_t * o_t, axis=0, keepdims=True)
        o_t = o_t * lax.rsqrt(ms + EPS) * subln_ref[...] * (1.0 - lam_init)
        o_ref[0, :, tile(c)] = _mxu_transpose(o_t.astype(BF16), eye_ref)


def _attn_diff(yc, diff_lambda, subln, eye, tq, tiles, lam_init):
    b, s, _ = yc.shape
    groups = H_DIFF // tiles
    width = tiles * LANES
    return pl.pallas_call(
        functools.partial(_attn_diff_kernel, tq=tq, lam_init=lam_init),
        grid=(b, groups, s // tq),
        in_specs=[pl.BlockSpec((1, tq, width), lambda bi, j, i: (bi, i, j)),
                  pl.BlockSpec((1, s, width), lambda bi, j, i: (bi, 0, groups + j)),
                  pl.BlockSpec((1, s, width), lambda bi, j, i: (bi, 0, 2 * groups + j)),
                  _const_spec(diff_lambda.shape), _const_spec(subln.shape), _const_spec(eye.shape)],
        out_specs=pl.BlockSpec((1, tq, width), lambda bi, j, i: (bi, i, j)),
        out_shape=jax.ShapeDtypeStruct((b, s, W_DIFF), BF16),
        scratch_shapes=[pltpu.VMEM((tiles, LANES + ONES_ROWS, s), BF16), pltpu.VMEM((2 * tiles, tq, LANES), BF16)]
        + _softmax_scratch(2 * tiles, tq, LANES + ONES_ROWS),
        compiler_params=_params(3),
        name="attn_diff",
    )(yc, yc, yc, diff_lambda, subln, eye)


def _pair_tile(x):
    return jnp.concatenate([x, x], axis=-1)


def kernel(x, p, positions, ffn1_norm, ffn1_wi, ffn1_wo, mix_norm, w_in, b_forget, qk_gain_fox, qk_gain_diff, diff_lambda, diff_subln, w_br, w_o, ffn2_norm, ffn2_wi, ffn2_wo, ple_norm, ple_gate_w, ple_proj_w):
    b, s, d = x.shape
    depth = p.shape[0]
    t = b * s
    tm = min(512, t)
    tq = min(ATTN_BLOCK, s)

    idx = jnp.arange(LANES)
    grp = jnp.where((idx[:, None] // HEAD_DIM) == (idx[None, :] // HEAD_DIM), 1.0 / HEAD_DIM, 0.0).astype(BF16)
    tri_incl = (idx[:, None] >= idx[None, :]).astype(BF16)
    eye = jnp.eye(max(tq, LANES), dtype=BF16)
    later = (idx[None, :] > idx[:, None]).astype(BF16)
    lu = jnp.concatenate([jnp.concatenate([later, later], axis=1), jnp.ones((ONES_ROWS, 2 * LANES), BF16)], axis=0)

    rc, rs1, rs2 = _rope_tables(positions, tm)
    w_proj_all, w_gate_all = _split_w_in(w_in, min(256, d))

    h = x.reshape(t, d)
    for i in range(depth):
        lam_init = 0.8 - 0.6 * math.exp(-0.3 * i)
        w_proj, w_gate = w_proj_all[i], w_gate_all[i]
        fb_bias = jnp.repeat(b_forget[i], LANES)[None]

        h = _ffn(h, ffn1_norm[i][None], ffn1_wi[i].astype(BF16), ffn1_wo[i].astype(BF16), tm)
        ya, yb, yc, fb = _inproj(h, mix_norm[i][None], w_proj, _pair_tile(qk_gain_fox[i]),
                                 _pair_tile(qk_gain_diff[i]), grp, rc, rs1, rs2, tm)
        negcum = _forget_cumsum(fb, fb_bias, tri_incl, s)
        oa = _attn_sb(ya.reshape(b, s, -1), lu, eye, tq, SB_PAIRS)
        ob = _attn_fox(yb.reshape(b, s, -1), negcum, eye, tq, FOX_TILES)
        oc = _attn_diff(yc.reshape(b, s, -1), diff_lambda[i], diff_subln[i][:, None], eye, tq, DIFF_TILES, lam_init)
        h = _merge(h, oa.reshape(t, -1), ob.reshape(t, -1), oc.reshape(t, -1), mix_norm[i][None],
                   w_gate, w_br[i].astype(BF16), w_o[i].astype(BF16), tm)
        h = _ffn(h, ffn2_norm[i][None], ffn2_wi[i].astype(BF16), ffn2_wo[i].astype(BF16), tm)
        h = _ple(h, p[i].reshape(t, P_DIM), ple_norm[i][None], ple_gate_w[i].astype(BF16),
                 ple_proj_w[i].astype(BF16), tm)
    return h.reshape(b, s, d)
```

```python
import functools
import math

import jax
import jax.numpy as jnp
from jax import lax
from jax.experimental import pallas as pl
from jax.experimental.pallas import tpu as pltpu

D_MODEL = 1024
HEAD_DIM = 64
LANES = 128
N_PAIR_SB = 2
N_PAIR_FOX = 2
H_DIFF = 4
W_SB = 256
W_FOX = 256
W_DIFF = 512
ROT_DIM = 16
ROPE_THETA = 500000.0
D_FF = 2816
FF_CHUNK = 256
ATTN_BLOCK = 256
SB_PAIRS = 2
FOX_TILES = 2
DIFF_TILES = 4
P_DIM = 256
EPS = 1e-6
CHUNK = 64
LOG2E = math.log2(math.e)
SCALE = HEAD_DIM ** -0.5 * LOG2E
FB_COLS = 2 * N_PAIR_FOX * LANES
IN_OFFSETS = {"a": 0, "b": 768, "fb": 1536, "c": 1540, "gates": 3076, "end": 6148}
VMEM_LIMIT = 56 * 1024 * 1024

F32 = jnp.float32
BF16 = jnp.bfloat16
NT_DIMS = (((1,), (1,)), ((), ()))


def _dot(a, b):
    return jnp.dot(a, b, preferred_element_type=F32)


def _dot_nt(a, b):
    return lax.dot_general(a, b, NT_DIMS, preferred_element_type=F32)


def _rms(x, gain):
    ms = jnp.mean(x * x, axis=-1, keepdims=True)
    return x * lax.rsqrt(ms + EPS) * gain


def _split_bf16(x, parts):
    out = []
    for _ in range(parts - 1):
        bits = lax.bitcast_convert_type(x, jnp.uint32) & jnp.uint32(0xFFFF0000)
        piece = lax.bitcast_convert_type(bits, F32)
        out.append(piece.astype(BF16))
        x = x - piece
    out.append(x.astype(BF16))
    return out


def _neg_abs(x):
    bits = lax.bitcast_convert_type(x, jnp.uint32) | jnp.uint32(0x80000000)
    return lax.bitcast_convert_type(bits, F32)


def _params(n_grid):
    return pltpu.CompilerParams(
        dimension_semantics=("arbitrary",) * n_grid, vmem_limit_bytes=VMEM_LIMIT)


def _const_spec(shape):
    zeros = (0,) * len(shape)
    return pl.BlockSpec(shape, lambda *_: zeros, pipeline_mode=pl.Buffered(1))


def _ffn_kernel(h_ref, gain_ref, wi_ref, wo_ref, o_ref, acc_ref):
    h = h_ref[...]
    xn = _rms(h, gain_ref[...]).astype(BF16)
    acc_ref[...] = jnp.zeros_like(acc_ref)
    for c in range(D_FF // FF_CHUNK):
        cols = slice(c * FF_CHUNK, (c + 1) * FF_CHUNK)
        a = _dot(xn, wi_ref[:, cols])
        g = _dot(xn, wi_ref[:, D_FF + c * FF_CHUNK:D_FF + (c + 1) * FF_CHUNK])
        act = (a * jax.nn.sigmoid(a) * g).astype(BF16)
        acc_ref[...] += _dot(act, wo_ref[cols, :])
    o_ref[...] = h + 0.5 * acc_ref[...]


def _ffn(h, gain, wi, wo, tm):
    t = h.shape[0]
    row = pl.BlockSpec((tm, D_MODEL), lambda i: (i, 0))
    return pl.pallas_call(
        _ffn_kernel,
        grid=(t // tm,),
        in_specs=[row, _const_spec(gain.shape), _const_spec(wi.shape), _const_spec(wo.shape)],
        out_specs=row,
        out_shape=jax.ShapeDtypeStruct(h.shape, F32),
        scratch_shapes=[pltpu.VMEM((tm, D_MODEL), F32)],
        compiler_params=_params(1),
        name="ffn",
    )(h, gain, wi, wo)


def _split_w_in_kernel(w_ref, wp_ref, wg_ref):
    o = IN_OFFSETS
    n_fb = o["c"] - o["fb"]
    wp_ref[0, :, 0:o["fb"]] = w_ref[0, :, 0:o["fb"]].astype(BF16)
    wp_ref[0, :, o["fb"]:o["gates"] - n_fb] = w_ref[0, :, o["c"]:o["gates"]].astype(BF16)
    for head in range(n_fb):
        col = w_ref[0, :, o["fb"] + head:o["fb"] + head + 1]
        start = o["gates"] - n_fb + head * LANES
        wp_ref[0, :, start:start + LANES] = jnp.broadcast_to(col, (col.shape[0], LANES)).astype(BF16)
    wg_ref[0] = w_ref[0, :, o["gates"]:o["end"]].astype(BF16)


def _split_w_in(w_in, rows):
    depth, d, cols = w_in.shape
    o = IN_OFFSETS
    n_proj = o["gates"] - (o["c"] - o["fb"]) + FB_COLS
    n_gate = o["end"] - o["gates"]
    blk = lambda n: pl.BlockSpec((1, rows, n), lambda l, r: (l, r, 0))
    return pl.pallas_call(
        _split_w_in_kernel,
        grid=(depth, d // rows),
        in_specs=[blk(cols)],
        out_specs=[blk(n_proj), blk(n_gate)],
        out_shape=[jax.ShapeDtypeStruct((depth, d, n_proj), BF16), jax.ShapeDtypeStruct((depth, d, n_gate), BF16)],
        compiler_params=_params(2),
        name="split_w_in",
    )(w_in)


def _rope_kernel(pos_ref, const_ref, c_ref, s1_ref, s2_ref):
    ang = pos_ref[...].astype(F32) * const_ref[0:1, :]
    sin = jnp.sin(ang)
    c_ref[...] = jnp.cos(ang)
    s1_ref[...] = sin * const_ref[1:2, :]
    s2_ref[...] = sin * const_ref[2:3, :]


def _rope_tables(positions, tm):
    t = positions.size
    lane = jnp.arange(LANES) % HEAD_DIM
    inv_freq = ROPE_THETA ** (-jnp.arange(0, ROT_DIM, 2, dtype=F32) / ROT_DIM)
    half = ROT_DIM // 2
    freq_lane = jnp.where(lane < ROT_DIM, inv_freq[lane % half], 0.0)
    plus = jnp.where((lane >= half) & (lane < ROT_DIM), 1.0, 0.0)
    minus = jnp.where(lane < half, -1.0, 0.0)
    const = jnp.zeros((8, LANES), F32).at[0].set(freq_lane).at[1].set(plus).at[2].set(minus)
    out = jax.ShapeDtypeStruct((t, LANES), F32)
    row = pl.BlockSpec((tm, LANES), lambda i: (i, 0))
    return pl.pallas_call(
        _rope_kernel,
        grid=(t // tm,),
        in_specs=[pl.BlockSpec((tm, 1), lambda i: (i, 0)), _const_spec(const.shape)],
        out_specs=[row, row, row],
        out_shape=[out, out, out],
        compiler_params=_params(1),
        name="rope_tables",
    )(positions.reshape(t, 1), const)


def _inproj_kernel(h_ref, gain_ref, w_ref, gfox_ref, gdiff_ref, grp_ref, rc_ref, rs1_ref, rs2_ref,
                   ya_ref, yb_ref, yc_ref, fb_ref):
    xn = _rms(h_ref[...], gain_ref[...]).astype(BF16)
    grp = grp_ref[...]

    def head_norm(x, gain):
        ms = _dot((x * x).astype(BF16), grp)
        return x * lax.rsqrt(ms + EPS) * gain

    def rope(x):
        return (x * rc_ref[...] + pltpu.roll(x, ROT_DIM // 2, 1) * rs1_ref[...]
                + pltpu.roll(x, LANES - ROT_DIM // 2, 1) * rs2_ref[...])

    y = _dot(xn, w_ref[:, 0:768])
    ya_ref[:, 0:W_SB] = (y[:, 0:W_SB] * SCALE).astype(BF16)
    ya_ref[:, W_SB:768] = y[:, W_SB:768].astype(BF16)

    y = _dot(xn, w_ref[:, 768:1536])
    for blk in range(2 * N_PAIR_FOX):
        is_q = blk < N_PAIR_FOX
        x = head_norm(y[:, blk * LANES:(blk + 1) * LANES], gfox_ref[(0 if is_q else 1):(1 if is_q else 2), :])
        if is_q:
            x = x * SCALE
        yb_ref[:, blk * LANES:(blk + 1) * LANES] = x.astype(BF16)
    yb_ref[:, 2 * W_FOX:768] = y[:, 2 * W_FOX:768].astype(BF16)

    y = _dot(xn, w_ref[:, 1536:3072])
    for blk in range(2 * H_DIFF):
        is_q = blk < H_DIFF
        x = head_norm(y[:, blk * LANES:(blk + 1) * LANES], gdiff_ref[(0 if is_q else 1):(1 if is_q else 2), :])
        x = rope(x)
        if is_q:
            x = x * SCALE
        yc_ref[:, blk * LANES:(blk + 1) * LANES] = x.astype(BF16)
    yc_ref[:, 2 * W_DIFF:1536] = y[:, 2 * W_DIFF:1536].astype(BF16)

    fb_ref[...] = _dot(xn, w_ref[:, 3072:3072 + FB_COLS])


def _inproj(h, gain, w, gfox, gdiff, grp, rc, rs1, rs2, tm):
    t = h.shape[0]
    row = lambda n: pl.BlockSpec((tm, n), lambda i: (i, 0))
    return pl.pallas_call(
        _inproj_kernel,
        grid=(t // tm,),
        in_specs=[row(D_MODEL), _const_spec(gain.shape), _const_spec(w.shape), _const_spec(gfox.shape),
                  _const_spec(gdiff.shape), _const_spec(grp.shape), row(LANES), row(LANES), row(LANES)],
        out_specs=[row(768), row(768), row(1536), row(FB_COLS)],
        out_shape=[jax.ShapeDtypeStruct((t, 768), BF16), jax.ShapeDtypeStruct((t, 768), BF16),
                   jax.ShapeDtypeStruct((t, 1536), BF16), jax.ShapeDtypeStruct((t, FB_COLS), F32)],
        compiler_params=_params(1),
        name="inproj",
    )(h, gain, w, gfox, gdiff, grp, rc, rs1, rs2)


def _forget_kernel(fb_ref, bias_ref, tri_ref, o_ref):
    s = fb_ref.shape[0]
    tri = tri_ref[...]
    carry = jnp.zeros((1, fb_ref.shape[1]), F32)
    for blk in range(s // LANES):
        x = fb_ref[blk * LANES:(blk + 1) * LANES, :] + bias_ref[...]
        log_f = jnp.minimum(x, 0.0) - jnp.log(1.0 + jnp.exp(-jnp.abs(x)))
        cum = carry
        for piece in _split_bf16(log_f, 3):
            cum = cum + _dot(tri, piece)
        o_ref[blk * LANES:(blk + 1) * LANES, :] = cum * -LOG2E
        carry = cum[LANES - 1:LANES, :]


def _forget_cumsum(fb, bias, tri, s):
    t, cols = fb.shape
    spec = pl.BlockSpec((s, cols), lambda i: (i, 0))
    return pl.pallas_call(
        _forget_kernel,
        grid=(t // s,),
        in_specs=[spec, _const_spec(bias.shape), _const_spec(tri.shape)],
        out_specs=spec,
        out_shape=jax.ShapeDtypeStruct(fb.shape, F32),
        compiler_params=_params(1),
        name="forget_cumsum",
    )(fb, bias, tri)


def _merge_kernel(h_ref, oa_ref, ob_ref, oc_ref, gain_ref, wgate_ref, wbr_ref, wo_ref, o_ref):
    h = h_ref[...]
    u = _rms(h, gain_ref[...]).astype(BF16)
    merged = None
    row0 = 0
    for br, o_br in enumerate((oa_ref, ob_ref, oc_ref)):
        width = o_br.shape[1]
        y = _dot(o_br[...], wbr_ref[row0:row0 + width, :])
        gate = jax.nn.sigmoid(_dot(u, wgate_ref[:, br * D_MODEL:(br + 1) * D_MODEL]))
        merged = gate * y if merged is None else merged + gate * y
        row0 += width
    o_ref[...] = h + _dot(merged.astype(BF16), wo_ref[...])


def _merge(h, oa, ob, oc, gain, wgate, wbr, wo, tm):
    t = h.shape[0]
    row = lambda n: pl.BlockSpec((tm, n), lambda i: (i, 0))
    return pl.pallas_call(
        _merge_kernel,
        grid=(t // tm,),
        in_specs=[row(D_MODEL), row(W_SB), row(W_FOX), row(W_DIFF), _const_spec(gain.shape),
                  _const_spec(wgate.shape), _const_spec(wbr.shape), _const_spec(wo.shape)],
        out_specs=row(D_MODEL),
        out_shape=jax.ShapeDtypeStruct(h.shape, F32),
        compiler_params=_params(1),
        name="merge",
    )(h, oa, ob, oc, gain, wgate, wbr, wo)


def _ple_kernel(h_ref, p_ref, gain_ref, wgate_ref, wproj_ref, o_ref):
    h = h_ref[...]
    u = _rms(h, gain_ref[...]).astype(BF16)
    gate = jax.nn.sigmoid(_dot(u, wgate_ref[...]))
    o_ref[...] = h + gate * _dot(p_ref[...].astype(BF16), wproj_ref[...])


def _ple(h, p, gain, wgate, wproj, tm):
    t = h.shape[0]
    row = lambda n: pl.BlockSpec((tm, n), lambda i: (i, 0))
    return pl.pallas_call(
        _ple_kernel,
        grid=(t // tm,),
        in_specs=[row(D_MODEL), row(P_DIM), _const_spec(gain.shape), _const_spec(wgate.shape),
                  _const_spec(wproj.shape)],
        out_specs=row(D_MODEL),
        out_shape=jax.ShapeDtypeStruct(h.shape, F32),
        compiler_params=_params(1),
        name="ple",
    )(h, p, gain, wgate, wproj)


TRANSPOSE_ROWS = 512
NO_BLOCK_LOGIT = -1e30


def _low_lanes():
    return lax.broadcasted_iota(jnp.int32, (1, LANES), 1) < HEAD_DIM


def _block_iota(tq, axis):
    return lax.broadcasted_iota(jnp.int32, (tq, tq), axis)


def _block_ds(kb, tq):
    if isinstance(kb, int):
        return pl.ds(kb * tq, tq)
    return pl.ds(pl.multiple_of(kb * tq, tq), tq)


def _mxu_transpose(x, eye_ref):
    n = x.shape[1]
    return _dot_nt(eye_ref[0:n, 0:n], x).astype(BF16)


def _for_each_transposed_chunk(src_ref, lanes, eye_ref, write):
    s = src_ref.shape[1]
    step = min(TRANSPOSE_ROWS, s)
    for start in range(0, s, step):
        pos = slice(start, start + step)
        write(pos, _mxu_transpose(src_ref[0, pos, lanes], eye_ref))


def _split_heads(q):
    low = _low_lanes()
    zero = jnp.zeros_like(q)
    return jnp.where(low, q, zero), jnp.where(low, zero, q)


def _store_split_heads(q_ref, qs_ref):
    for c in range(q_ref.shape[2] // LANES):
        for h, qh in enumerate(_split_heads(q_ref[0, :, c * LANES:(c + 1) * LANES])):
            qs_ref[2 * c + h] = qh


def _attn_sb_kernel(q_ref, k_ref, v_ref, lu_ref, eye_ref, o_ref, vt_ref, qs_ref, lb_ref, lk_ref, a_ref, acc_ref,
                    run_ref, *, tq):
    i = pl.program_id(2)
    pairs = q_ref.shape[2] // LANES

    def tile(c):
        return slice(c * LANES, (c + 1) * LANES)

    @pl.when(i == 0)
    def _():
        for c in range(pairs):
            def write(pos, vt, c=c):
                vt_ref[tile(c), pos] = vt

            _for_each_transposed_chunk(v_ref, tile(c), eye_ref, write)

    _store_split_heads(q_ref, qs_ref)
    strict = _block_iota(tq, 0) < _block_iota(tq, 1)
    lu = lu_ref[...]
    heads = range(2 * pairs)
    n_tiles = tq // LANES

    def logits(kb):
        return [_dot_nt(k_ref[0, _block_ds(kb, tq), tile(h // 2)], qs_ref[h]) for h in heads]

    def logits_stage(zs, slot, diag):
        for h in heads:
            z = zs[h]
            log_beta = jnp.minimum(z, 0.0) - jnp.log2(1.0 + jnp.exp2(_neg_abs(z)))
            log_keep = log_beta - z
            if diag:
                log_keep = jnp.where(strict, log_keep, 0.0)
                log_beta = jnp.where(strict, log_beta, -jnp.inf)
            lb_ref[slot, h] = log_beta
            hi, lo = _split_bf16(log_keep, 2)
            for c in range(n_tiles):
                lk_ref[slot, h, 2 * c * LANES:(2 * c + 1) * LANES, :] = hi[c * LANES:(c + 1) * LANES, :]
                lk_ref[slot, h, (2 * c + 1) * LANES:(2 * c + 2) * LANES, :] = lo[c * LANES:(c + 1) * LANES, :]

    def suffix_sums(slot):
        return [[_dot(lu, lk_ref[slot, h, 2 * c * LANES:(2 * c + 2) * LANES, :]) for c in range(n_tiles)]
                for h in heads]

    def weights_stage(slot, h, sums, run):
        for c in reversed(range(n_tiles)):
            sl = slice(c * LANES, (c + 1) * LANES)
            a_ref[slot, h, sl, :] = jnp.exp2(lb_ref[slot, h, sl, :] + sums[c][0:LANES, :] + run).astype(BF16)
            run = run + sums[c][LANES:LANES + 1, :]
        return run

    def products(slot, kb):
        return [_dot(vt_ref[h * HEAD_DIM:(h + 1) * HEAD_DIM, _block_ds(kb, tq)], a_ref[slot, h]) for h in heads]

    def accumulate(pv):
        for h in heads:
            acc_ref[h] += pv[h]

    def step(n, cur):
        kb = i - n
        zs = logits(jnp.maximum(kb - 1, 0))
        if cur == 0:
            zs = [jnp.where(kb >= 1, z, NO_BLOCK_LOGIT) for z in zs]
        sums = suffix_sums(cur)
        pv = products(1 - cur, jnp.minimum(kb + 1, i))
        for h in heads:
            run_ref[h] = weights_stage(cur, h, sums[h], run_ref[h])
        logits_stage(zs, 1 - cur, False)
        accumulate(pv)

    def two_steps(t, carry):
        step(2 * t, 0)
        step(2 * t + 1, 1)
        return carry

    logits_stage(logits(i), 0, True)
    a_ref[1] = jnp.zeros(a_ref.shape[1:], BF16)
    acc_ref[...] = jnp.zeros_like(acc_ref)
    run_ref[...] = jnp.zeros_like(run_ref)
    lax.fori_loop(0, i // 2 + 1, two_steps, 0)
    accumulate(products(1, 0))
    for c in range(pairs):
        o_t = jnp.concatenate([acc_ref[2 * c], acc_ref[2 * c + 1]], axis=0).astype(BF16)
        o_ref[0, :, tile(c)] = _mxu_transpose(o_t, eye_ref)


def _attn_sb(ya, lu, eye, tq, pairs):
    b, s, _ = ya.shape
    groups = N_PAIR_SB // pairs
    width = pairs * LANES
    heads = 2 * pairs
    return pl.pallas_call(
        functools.partial(_attn_sb_kernel, tq=tq),
        grid=(b, groups, s // tq),
        in_specs=[pl.BlockSpec((1, tq, width), lambda bi, j, i: (bi, i, j)),
                  pl.BlockSpec((1, s, width), lambda bi, j, i: (bi, 0, groups + j)),
                  pl.BlockSpec((1, s, width), lambda bi, j, i: (bi, 0, 2 * groups + j)),
                  _const_spec(lu.shape), _const_spec(eye.shape)],
        out_specs=pl.BlockSpec((1, tq, width), lambda bi, j, i: (bi, i, j)),
        out_shape=jax.ShapeDtypeStruct((b, s, W_SB), BF16),
        scratch_shapes=[pltpu.VMEM((width, s), BF16), pltpu.VMEM((heads, tq, LANES), BF16),
                        pltpu.VMEM((2, heads, tq, tq), F32), pltpu.VMEM((2, heads, 2 * tq, tq), BF16),
                        pltpu.VMEM((2, heads, tq, tq), BF16), pltpu.VMEM((heads, HEAD_DIM, tq), F32),
                        pltpu.VMEM((heads, 1, tq), F32)],
        compiler_params=_params(3),
        name="attn_stickbreak",
    )(ya, ya, ya, lu, eye)


def _softmax_scratch(n_streams, tq, acc_rows):
    return [pltpu.VMEM((2, n_streams, tq, tq), F32), pltpu.VMEM((2, n_streams, tq, tq), BF16),
            pltpu.VMEM((n_streams, acc_rows, tq), F32), pltpu.VMEM((n_streams, 1, tq), F32)]


def _softmax_walk(i, n_st, visible, score, value, s_ref, p_ref, acc_ref, stat_ref, tq):
    streams = range(n_st)
    q_tiles = range(tq // LANES)

    def step(n, cur):
        kb = i - n
        for st in streams:
            s_next = score(st, jnp.maximum(kb - 1, 0))
            s_ref[1 - cur, st] = jnp.where(kb >= 1, s_next, -jnp.inf) if cur == 0 else s_next
        for st in streams:
            alpha = []
            for j in q_tiles:
                cols = slice(j * LANES, (j + 1) * LANES)
                s = s_ref[cur, st, :, cols]
                m_old = stat_ref[st, :, cols]
                m_j = jnp.maximum(m_old, jnp.max(s, axis=0, keepdims=True))
                p_ref[cur, st, :, cols] = jnp.exp2(s - m_j).astype(BF16)
                stat_ref[st, :, cols] = m_j
                alpha.append(jnp.exp2(m_old - m_j))
            pv = _dot(value(st, jnp.maximum(kb, 0)), p_ref[cur, st])
            acc_ref[st] = acc_ref[st] * jnp.concatenate(alpha, axis=1) + pv

    def two_steps(t, carry):
        step(2 * t, 0)
        step(2 * t + 1, 1)
        return carry

    for st in streams:
        s_ref[0, st] = jnp.where(visible, score(st, i), -jnp.inf)
    acc_ref[...] = jnp.zeros_like(acc_ref)
    stat_ref[...] = jnp.full(stat_ref.shape, -jnp.inf, F32)
    lax.fori_loop(0, i // 2 + 1, two_steps, 0)
    return [acc_ref[st] for st in streams]


ONES_ROWS = 16


def _attn_fox_kernel(q_ref, k_ref, v_ref, nc_ref, eye_ref, o_ref, vt_ref, qs_ref, s_ref, p_ref, acc_ref,
                     stat_ref, *, tq):
    i = pl.program_id(2)
    tiles = q_ref.shape[2] // LANES

    def tile(c):
        return slice(c * LANES, (c + 1) * LANES)

    @pl.when(i == 0)
    def _():
        for c in range(tiles):
            def write(pos, vt, c=c):
                for h in range(2):
                    vt_ref[2 * c + h, 0:HEAD_DIM, pos] = vt[h * HEAD_DIM:(h + 1) * HEAD_DIM]

            _for_each_transposed_chunk(v_ref, tile(c), eye_ref, write)
        vt_ref[:, HEAD_DIM:, :] = jnp.ones((2 * tiles, ONES_ROWS, vt_ref.shape[2]), BF16)

    _store_split_heads(q_ref, qs_ref)

    def score(st, kb):
        bias = nc_ref[_block_ds(kb, tq), tile(st)]
        s = _dot_nt(k_ref[0, _block_ds(kb, tq), tile(st // 2)], qs_ref[st])
        return s + jnp.concatenate([bias] * (tq // LANES), axis=1)

    def value(st, kb):
        return vt_ref[st, :, _block_ds(kb, tq)]

    causal = _block_iota(tq, 0) <= _block_iota(tq, 1)
    acc = _softmax_walk(i, 2 * tiles, causal, score, value, s_ref, p_ref, acc_ref, stat_ref, tq)
    outs = [a[0:HEAD_DIM] / a[HEAD_DIM:HEAD_DIM + 1] for a in acc]
    for c in range(tiles):
        o_t = jnp.concatenate(outs[2 * c:2 * c + 2], axis=0).astype(BF16)
        o_ref[0, :, tile(c)] = _mxu_transpose(o_t, eye_ref)


def _attn_fox(yb, negcum, eye, tq, tiles):
    b, s, _ = yb.shape
    groups = N_PAIR_FOX // tiles
    width = tiles * LANES
    return pl.pallas_call(
        functools.partial(_attn_fox_kernel, tq=tq),
        grid=(b, groups, s // tq),
        in_specs=[pl.BlockSpec((1, tq, width), lambda bi, j, i: (bi, i, j)),
                  pl.BlockSpec((1, s, width), lambda bi, j, i: (bi, 0, groups + j)),
                  pl.BlockSpec((1, s, width), lambda bi, j, i: (bi, 0, 2 * groups + j)),
                  pl.BlockSpec((s, 2 * width), lambda bi, j, i: (bi, j)), _const_spec(eye.shape)],
        out_specs=pl.BlockSpec((1, tq, width), lambda bi, j, i: (bi, i, j)),
        out_shape=jax.ShapeDtypeStruct((b, s, W_FOX), BF16),
        scratch_shapes=[pltpu.VMEM((2 * tiles, HEAD_DIM + ONES_ROWS, s), BF16),
                        pltpu.VMEM((2 * tiles, tq, LANES), BF16)]
        + _softmax_scratch(2 * tiles, tq, HEAD_DIM + ONES_ROWS),
        compiler_params=_params(3),
        name="attn_forget",
    )(yb, yb, yb, negcum, eye)


def _attn_diff_kernel(q_ref, k_ref, v_ref, lam_ref, subln_ref, eye_ref, o_ref, vt_ref, qs_ref, s_ref, p_ref,
                      acc_ref, stat_ref, *, tq, lam_init):
    i = pl.program_id(2)
    tiles = q_ref.shape[2] // LANES

    def tile(c):
        return slice(c * LANES, (c + 1) * LANES)

    @pl.when(i == 0)
    def _():
        for c in range(tiles):
            def write(pos, vt, c=c):
                vt_ref[c, 0:LANES, pos] = vt

            _for_each_transposed_chunk(v_ref, tile(c), eye_ref, write)
        vt_ref[:, LANES:, :] = jnp.ones((tiles, ONES_ROWS, vt_ref.shape[2]), BF16)

    _store_split_heads(q_ref, qs_ref)

    def score(st, kb):
        return _dot_nt(k_ref[0, _block_ds(kb, tq), tile(st // 2)], qs_ref[st])

    def value(st, kb):
        return vt_ref[st // 2, :, _block_ds(kb, tq)]

    visible = (_block_iota(tq, 0) // CHUNK) <= (_block_iota(tq, 1) // CHUNK)
    acc = _softmax_walk(i, 2 * tiles, visible, score, value, s_ref, p_ref, acc_ref, stat_ref, tq)

    lf = lam_ref[...]
    lam = (jnp.exp(jnp.sum(lf[0:1] * lf[1:2], axis=1, keepdims=True))
           - jnp.exp(jnp.sum(lf[2:3] * lf[3:4], axis=1, keepdims=True)) + lam_init)
    for c in range(tiles):
        a1, a2 = (a[0:LANES] / a[LANES:LANES + 1] for a in acc[2 * c:2 * c + 2])
        o_t = a1 - lam * a2
        ms = jnp.mean(o_t * o_t, axis=0, keepdims=True)
        o_t = o_t * lax.rsqrt(ms + EPS) * subln_ref[...] * (1.0 - lam_init)
        o_ref[0, :, tile(c)] = _mxu_transpose(o_t.astype(BF16), eye_ref)


def _attn_diff(yc, diff_lambda, subln, eye, tq, tiles, lam_init):
    b, s, _ = yc.shape
    groups = H_DIFF // tiles
    width = tiles * LANES
    return pl.pallas_call(
        functools.partial(_attn_diff_kernel, tq=tq, lam_init=lam_init),
        grid=(b, groups, s // tq),
        in_specs=[pl.BlockSpec((1, tq, width), lambda bi, j, i: (bi, i, j)),
                  pl.BlockSpec((1, s, width), lambda bi, j, i: (bi, 0, groups + j)),
                  pl.BlockSpec((1, s, width), lambda bi, j, i: (bi, 0, 2 * groups + j)),
                  _const_spec(diff_lambda.shape), _const_spec(subln.shape), _const_spec(eye.shape)],
        out_specs=pl.BlockSpec((1, tq, width), lambda bi, j, i: (bi, i, j)),
        out_shape=jax.ShapeDtypeStruct((b, s, W_DIFF), BF16),
        scratch_shapes=[pltpu.VMEM((tiles, LANES + ONES_ROWS, s), BF16), pltpu.VMEM((2 * tiles, tq, LANES), BF16)]
        + _softmax_scratch(2 * tiles, tq, LANES + ONES_ROWS),
        compiler_params=_params(3),
        name="attn_diff",
    )(yc, yc, yc, diff_lambda, subln, eye)


def _pair_tile(x):
    return jnp.concatenate([x, x], axis=-1)


def kernel(x, p, positions, ffn1_norm, ffn1_wi, ffn1_wo, mix_norm, w_in, b_forget, qk_gain_fox, qk_gain_diff, diff_lambda, diff_subln, w_br, w_o, ffn2_norm, ffn2_wi, ffn2_wo, ple_norm, ple_gate_w, ple_proj_w):
    b, s, d = x.shape
    depth = p.shape[0]
    t = b * s
    tm = min(512, t)
    tq = min(ATTN_BLOCK, s)

    idx = jnp.arange(LANES)
    grp = jnp.where((idx[:, None] // HEAD_DIM) == (idx[None, :] // HEAD_DIM), 1.0 / HEAD_DIM, 0.0).astype(BF16)
    tri_incl = (idx[:, None] >= idx[None, :]).astype(BF16)
    eye = jnp.eye(max(tq, LANES), dtype=BF16)
    later = (idx[None, :] > idx[:, None]).astype(BF16)
    lu = jnp.concatenate([jnp.concatenate([later, later], axis=1), jnp.ones((ONES_ROWS, 2 * LANES), BF16)], axis=0)

    rc, rs1, rs2 = _rope_tables(positions, tm)
    w_proj_all, w_gate_all = _split_w_in(w_in, min(256, d))

    h = x.reshape(t, d)
    for i in range(depth):
        lam_init = 0.8 - 0.6 * math.exp(-0.3 * i)
        w_proj, w_gate = w_proj_all[i], w_gate_all[i]
        fb_bias = jnp.repeat(b_forget[i], LANES)[None]

        h = _ffn(h, ffn1_norm[i][None], ffn1_wi[i].astype(BF16), ffn1_wo[i].astype(BF16), tm)
        ya, yb, yc, fb = _inproj(h, mix_norm[i][None], w_proj, _pair_tile(qk_gain_fox[i]),
                                 _pair_tile(qk_gain_diff[i]), grp, rc, rs1, rs2, tm)
        negcum = _forget_cumsum(fb, fb_bias, tri_incl, s)
        oa = _attn_sb(ya.reshape(b, s, -1), lu, eye, tq, SB_PAIRS)
        ob = _attn_fox(yb.reshape(b, s, -1), negcum, eye, tq, FOX_TILES)
        oc = _attn_diff(yc.reshape(b, s, -1), diff_lambda[i], diff_subln[i][:, None], eye, tq, DIFF_TILES, lam_init)
        h = _merge(h, oa.reshape(t, -1), ob.reshape(t, -1), oc.reshape(t, -1), mix_norm[i][None],
                   w_gate, w_br[i].astype(BF16), w_o[i].astype(BF16), tm)
        h = _ffn(h, ffn2_norm[i][None], ffn2_wi[i].astype(BF16), ffn2_wo[i].astype(BF16), tm)
        h = _ple(h, p[i].reshape(t, P_DIM), ple_norm[i][None], ple_gate_w[i].astype(BF16),
                 ple_proj_w[i].astype(BF16), tm)
    return h.reshape(b, s, d)
```

```python
import functools
import math

import jax
import jax.numpy as jnp
from jax import lax
from jax.experimental import pallas as pl
from jax.experimental.pallas import tpu as pltpu

D_MODEL = 1024
HEAD_DIM = 64
LANES = 128
N_PAIR_SB = 2
N_PAIR_FOX = 2
H_DIFF = 4
W_SB = 256
W_FOX = 256
W_DIFF = 512
ROT_DIM = 16
ROPE_THETA = 500000.0
D_FF = 2816
FF_CHUNK = 256
ATTN_BLOCK = 256
SB_PAIRS = 2
FOX_TILES = 2
DIFF_TILES = 4
P_DIM = 256
EPS = 1e-6
CHUNK = 64
LOG2E = math.log2(math.e)
SCALE = HEAD_DIM ** -0.5 * LOG2E
FB_COLS = 2 * N_PAIR_FOX * LANES
IN_OFFSETS = {"a": 0, "b": 768, "fb": 1536, "c": 1540, "gates": 3076, "end": 6148}
VMEM_LIMIT = 56 * 1024 * 1024

F32 = jnp.float32
BF16 = jnp.bfloat16
NT_DIMS = (((1,), (1,)), ((), ()))


def _dot(a, b):
    return jnp.dot(a, b, preferred_element_type=F32)


def _dot_nt(a, b):
    return lax.dot_general(a, b, NT_DIMS, preferred_element_type=F32)


def _rms(x, gain):
    ms = jnp.mean(x * x, axis=-1, keepdims=True)
    return x * lax.rsqrt(ms + EPS) * gain


def _split_bf16(x, parts):
    out = []
    for _ in range(parts - 1):
        bits = lax.bitcast_convert_type(x, jnp.uint32) & jnp.uint32(0xFFFF0000)
        piece = lax.bitcast_convert_type(bits, F32)
        out.append(piece.astype(BF16))
        x = x - piece
    out.append(x.astype(BF16))
    return out


def _neg_abs(x):
    bits = lax.bitcast_convert_type(x, jnp.uint32) | jnp.uint32(0x80000000)
    return lax.bitcast_convert_type(bits, F32)


def _params(n_grid):
    return pltpu.CompilerParams(
        dimension_semantics=("arbitrary",) * n_grid, vmem_limit_bytes=VMEM_LIMIT)


def _const_spec(shape):
    zeros = (0,) * len(shape)
    return pl.BlockSpec(shape, lambda *_: zeros, pipeline_mode=pl.Buffered(1))


def _ffn_kernel(h_ref, gain_ref, wi_ref, wo_ref, o_ref, acc_ref):
    h = h_ref[...]
    xn = _rms(h, gain_ref[...]).astype(BF16)
    acc_ref[...] = jnp.zeros_like(acc_ref)
    for c in range(D_FF // FF_CHUNK):
        cols = slice(c * FF_CHUNK, (c + 1) * FF_CHUNK)
        a = _dot(xn, wi_ref[:, cols])
        g = _dot(xn, wi_ref[:, D_FF + c * FF_CHUNK:D_FF + (c + 1) * FF_CHUNK])
        act = (a * jax.nn.sigmoid(a) * g).astype(BF16)
        acc_ref[...] += _dot(act, wo_ref[cols, :])
    o_ref[...] = h + 0.5 * acc_ref[...]


def _ffn(h, gain, wi, wo, tm):
    t = h.shape[0]
    row = pl.BlockSpec((tm, D_MODEL), lambda i: (i, 0))
    return pl.pallas_call(
        _ffn_kernel,
        grid=(t // tm,),
        in_specs=[row, _const_spec(gain.shape), _const_spec(wi.shape), _const_spec(wo.shape)],
        out_specs=row,
        out_shape=jax.ShapeDtypeStruct(h.shape, F32),
        scratch_shapes=[pltpu.VMEM((tm, D_MODEL), F32)],
        compiler_params=_params(1),
        name="ffn",
    )(h, gain, wi, wo)


def _split_w_in_kernel(w_ref, wp_ref, wg_ref):
    o = IN_OFFSETS
    n_fb = o["c"] - o["fb"]
    wp_ref[0, :, 0:o["fb"]] = w_ref[0, :, 0:o["fb"]].astype(BF16)
    wp_ref[0, :, o["fb"]:o["gates"] - n_fb] = w_ref[0, :, o["c"]:o["gates"]].astype(BF16)
    for head in range(n_fb):
        col = w_ref[0, :, o["fb"] + head:o["fb"] + head + 1]
        start = o["gates"] - n_fb + head * LANES
        wp_ref[0, :, start:start + LANES] = jnp.broadcast_to(col, (col.shape[0], LANES)).astype(BF16)
    wg_ref[0] = w_ref[0, :, o["gates"]:o["end"]].astype(BF16)


def _split_w_in(w_in, rows):
    depth, d, cols = w_in.shape
    o = IN_OFFSETS
    n_proj = o["gates"] - (o["c"] - o["fb"]) + FB_COLS
    n_gate = o["end"] - o["gates"]
    blk = lambda n: pl.BlockSpec((1, rows, n), lambda l, r: (l, r, 0))
    return pl.pallas_call(
        _split_w_in_kernel,
        grid=(depth, d // rows),
        in_specs=[blk(cols)],
        out_specs=[blk(n_proj), blk(n_gate)],
        out_shape=[jax.ShapeDtypeStruct((depth, d, n_proj), BF16), jax.ShapeDtypeStruct((depth, d, n_gate), BF16)],
        compiler_params=_params(2),
        name="split_w_in",
    )(w_in)


def _rope_kernel(pos_ref, const_ref, c_ref, s1_ref, s2_ref):
    ang = pos_ref[...].astype(F32) * const_ref[0:1, :]
    sin = jnp.sin(ang)
    c_ref[...] = jnp.cos(ang)
    s1_ref[...] = sin * const_ref[1:2, :]
    s2_ref[...] = sin * const_ref[2:3, :]


def _rope_tables(positions, tm):
    t = positions.size
    lane = jnp.arange(LANES) % HEAD_DIM
    inv_freq = ROPE_THETA ** (-jnp.arange(0, ROT_DIM, 2, dtype=F32) / ROT_DIM)
    half = ROT_DIM // 2
    freq_lane = jnp.where(lane < ROT_DIM, inv_freq[lane % half], 0.0)
    plus = jnp.where((lane >= half) & (lane < ROT_DIM), 1.0, 0.0)
    minus = jnp.where(lane < half, -1.0, 0.0)
    const = jnp.zeros((8, LANES), F32).at[0].set(freq_lane).at[1].set(plus).at[2].set(minus)
    out = jax.ShapeDtypeStruct((t, LANES), F32)
    row = pl.BlockSpec((tm, LANES), lambda i: (i, 0))
    return pl.pallas_call(
        _rope_kernel,
        grid=(t // tm,),
        in_specs=[pl.BlockSpec((tm, 1), lambda i: (i, 0)), _const_spec(const.shape)],
        out_specs=[row, row, row],
        out_shape=[out, out, out],
        compiler_params=_params(1),
        name="rope_tables",
    )(positions.reshape(t, 1), const)


def _inproj_kernel(h_ref, gain_ref, w_ref, gfox_ref, gdiff_ref, grp_ref, rc_ref, rs1_ref, rs2_ref,
                   ya_ref, yb_ref, yc_ref, fb_ref):
    xn = _rms(h_ref[...], gain_ref[...]).astype(BF16)
    grp = grp_ref[...]

    def head_norm(x, gain):
        ms = _dot((x * x).astype(BF16), grp)
        return x * lax.rsqrt(ms + EPS) * gain

    def rope(x):
        return (x * rc_ref[...] + pltpu.roll(x, ROT_DIM // 2, 1) * rs1_ref[...]
                + pltpu.roll(x, LANES - ROT_DIM // 2, 1) * rs2_ref[...])

    y = _dot(xn, w_ref[:, 0:768])
    ya_ref[:, 0:W_SB] = (y[:, 0:W_SB] * SCALE).astype(BF16)
    ya_ref[:, W_SB:768] = y[:, W_SB:768].astype(BF16)

    y = _dot(xn, w_ref[:, 768:1536])
    for blk in range(2 * N_PAIR_FOX):
        is_q = blk < N_PAIR_FOX
        x = head_norm(y[:, blk * LANES:(blk + 1) * LANES], gfox_ref[(0 if is_q else 1):(1 if is_q else 2), :])
        if is_q:
            x = x * SCALE
        yb_ref[:, blk * LANES:(blk + 1) * LANES] = x.astype(BF16)
    yb_ref[:, 2 * W_FOX:768] = y[:, 2 * W_FOX:768].astype(BF16)

    y = _dot(xn, w_ref[:, 1536:3072])
    for blk in range(2 * H_DIFF):
        is_q = blk < H_DIFF
        x = head_norm(y[:, blk * LANES:(blk + 1) * LANES], gdiff_ref[(0 if is_q else 1):(1 if is_q else 2), :])
        x = rope(x)
        if is_q:
            x = x * SCALE
        yc_ref[:, blk * LANES:(blk + 1) * LANES] = x.astype(BF16)
    yc_ref[:, 2 * W_DIFF:1536] = y[:, 2 * W_DIFF:1536].astype(BF16)

    fb_ref[...] = _dot(xn, w_ref[:, 3072:3072 + FB_COLS])


def _inproj(h, gain, w, gfox, gdiff, grp, rc, rs1, rs2, tm):
    t = h.shape[0]
    row = lambda n: pl.BlockSpec((tm, n), lambda i: (i, 0))
    return pl.pallas_call(
        _inproj_kernel,
        grid=(t // tm,),
        in_specs=[row(D_MODEL), _const_spec(gain.shape), _const_spec(w.shape), _const_spec(gfox.shape),
                  _const_spec(gdiff.shape), _const_spec(grp.shape), row(LANES), row(LANES), row(LANES)],
        out_specs=[row(768), row(768), row(1536), row(FB_COLS)],
        out_shape=[jax.ShapeDtypeStruct((t, 768), BF16), jax.ShapeDtypeStruct((t, 768), BF16),
                   jax.ShapeDtypeStruct((t, 1536), BF16), jax.ShapeDtypeStruct((t, FB_COLS), F32)],
        compiler_params=_params(1),
        name="inproj",
    )(h, gain, w, gfox, gdiff, grp, rc, rs1, rs2)


def _forget_kernel(fb_ref, bias_ref, tri_ref, o_ref):
    s = fb_ref.shape[0]
    tri = tri_ref[...]
    carry = jnp.zeros((1, fb_ref.shape[1]), F32)
    for blk in range(s // LANES):
        x = fb_ref[blk * LANES:(blk + 1) * LANES, :] + bias_ref[...]
        log_f = jnp.minimum(x, 0.0) - jnp.log(1.0 + jnp.exp(-jnp.abs(x)))
        cum = carry
        for piece in _split_bf16(log_f, 3):
            cum = cum + _dot(tri, piece)
        o_ref[blk * LANES:(blk + 1) * LANES, :] = cum * -LOG2E
        carry = cum[LANES - 1:LANES, :]


def _forget_cumsum(fb, bias, tri, s):
    t, cols = fb.shape
    spec = pl.BlockSpec((s, cols), lambda i: (i, 0))
    return pl.pallas_call(
        _forget_kernel,
        grid=(t // s,),
        in_specs=[spec, _const_spec(bias.shape), _const_spec(tri.shape)],
        out_specs=spec,
        out_shape=jax.ShapeDtypeStruct(fb.shape, F32),
        compiler_params=_params(1),
        name="forget_cumsum",
    )(fb, bias, tri)


def _merge_kernel(h_ref, oa_ref, ob_ref, oc_ref, gain_ref, wgate_ref, wbr_ref, wo_ref, o_ref):
    h = h_ref[...]
    u = _rms(h, gain_ref[...]).astype(BF16)
    merged = None
    row0 = 0
    for br, o_br in enumerate((oa_ref, ob_ref, oc_ref)):
        width = o_br.shape[1]
        y = _dot(o_br[...], wbr_ref[row0:row0 + width, :])
        gate = jax.nn.sigmoid(_dot(u, wgate_ref[:, br * D_MODEL:(br + 1) * D_MODEL]))
        merged = gate * y if merged is None else merged + gate * y
        row0 += width
    o_ref[...] = h + _dot(merged.astype(BF16), wo_ref[...])


def _merge(h, oa, ob, oc, gain, wgate, wbr, wo, tm):
    t = h.shape[0]
    row = lambda n: pl.BlockSpec((tm, n), lambda i: (i, 0))
    return pl.pallas_call(
        _merge_kernel,
        grid=(t // tm,),
        in_specs=[row(D_MODEL), row(W_SB), row(W_FOX), row(W_DIFF), _const_spec(gain.shape),
                  _const_spec(wgate.shape), _const_spec(wbr.shape), _const_spec(wo.shape)],
        out_specs=row(D_MODEL),
        out_shape=jax.ShapeDtypeStruct(h.shape, F32),
        compiler_params=_params(1),
        name="merge",
    )(h, oa, ob, oc, gain, wgate, wbr, wo)


def _ple_kernel(h_ref, p_ref, gain_ref, wgate_ref, wproj_ref, o_ref):
    h = h_ref[...]
    u = _rms(h, gain_ref[...]).astype(BF16)
    gate = jax.nn.sigmoid(_dot(u, wgate_ref[...]))
    o_ref[...] = h + gate * _dot(p_ref[...].astype(BF16), wproj_ref[...])


def _ple(h, p, gain, wgate, wproj, tm):
    t = h.shape[0]
    row = lambda n: pl.BlockSpec((tm, n), lambda i: (i, 0))
    return pl.pallas_call(
        _ple_kernel,
        grid=(t // tm,),
        in_specs=[row(D_MODEL), row(P_DIM), _const_spec(gain.shape), _const_spec(wgate.shape),
                  _const_spec(wproj.shape)],
        out_specs=row(D_MODEL),
        out_shape=jax.ShapeDtypeStruct(h.shape, F32),
        compiler_params=_params(1),
        name="ple",
    )(h, p, gain, wgate, wproj)


TRANSPOSE_ROWS = 512
NO_BLOCK_LOGIT = -1e30


def _low_lanes():
    return lax.broadcasted_iota(jnp.int32, (1, LANES), 1) < HEAD_DIM


def _block_iota(tq, axis):
    return lax.broadcasted_iota(jnp.int32, (tq, tq), axis)


def _block_ds(kb, tq):
    if isinstance(kb, int):
        return pl.ds(kb * tq, tq)
    return pl.ds(pl.multiple_of(kb * tq, tq), tq)


def _mxu_transpose(x, eye_ref):
    n = x.shape[1]
    return _dot_nt(eye_ref[0:n, 0:n], x).astype(BF16)


def _for_each_transposed_chunk(src_ref, lanes, eye_ref, write):
    s = src_ref.shape[1]
    step = min(TRANSPOSE_ROWS, s)
    for start in range(0, s, step):
        pos = slice(start, start + step)
        write(pos, _mxu_transpose(src_ref[0, pos, lanes], eye_ref))


def _split_heads(q):
    low = _low_lanes()
    zero = jnp.zeros_like(q)
    return jnp.where(low, q, zero), jnp.where(low, zero, q)


def _store_split_heads(q_ref, qs_ref):
    for c in range(q_ref.shape[2] // LANES):
        for h, qh in enumerate(_split_heads(q_ref[0, :, c * LANES:(c + 1) * LANES])):
            qs_ref[2 * c + h] = qh


def _attn_sb_kernel(q_ref, k_ref, v_ref, lu_ref, eye_ref, o_ref, vt_ref, qs_ref, lb_ref, lk_ref, a_ref, acc_ref,
                    run_ref, *, tq):
    i = pl.program_id(2)
    pairs = q_ref.shape[2] // LANES

    def tile(c):
        return slice(c * LANES, (c + 1) * LANES)

    @pl.when(i == 0)
    def _():
        for c in range(pairs):
            def write(pos, vt, c=c):
                vt_ref[tile(c), pos] = vt

            _for_each_transposed_chunk(v_ref, tile(c), eye_ref, write)

    _store_split_heads(q_ref, qs_ref)
    strict = _block_iota(tq, 0) < _block_iota(tq, 1)
    lu = lu_ref[...]
    heads = range(2 * pairs)

    def logits(kb):
        return [_dot_nt(k_ref[0, _block_ds(kb, tq), tile(h // 2)], qs_ref[h]) for h in heads]

    def logits_stage(zs, slot, diag):
        for h in heads:
            z = zs[h]
            log_beta = jnp.minimum(z, 0.0) - jnp.log2(1.0 + jnp.exp2(_neg_abs(z)))
            log_keep = log_beta - z
            if diag:
                log_keep = jnp.where(strict, log_keep, 0.0)
                log_beta = jnp.where(strict, log_beta, -jnp.inf)
            lb_ref[slot, h] = log_beta
            lk_ref[slot, h] = log_keep.astype(BF16)

    def suffix_sums(slot):
        return [_dot(lu, lk_ref[slot, h]) for h in heads]

    def weights_stage(slot, h, sums, run):
        a_ref[slot, h] = jnp.exp2(lb_ref[slot, h] + sums[0:tq, :] + run).astype(BF16)
        return run + sums[tq:tq + 1, :]

    def products(slot, kb):
        return [_dot(vt_ref[h * HEAD_DIM:(h + 1) * HEAD_DIM, _block_ds(kb, tq)], a_ref[slot, h]) for h in heads]

    def accumulate(pv):
        for h in heads:
            acc_ref[h] += pv[h]

    def step(n, cur):
        kb = i - n
        zs = logits(jnp.maximum(kb - 1, 0))
        if cur == 0:
            zs = [jnp.where(kb >= 1, z, NO_BLOCK_LOGIT) for z in zs]
        sums = suffix_sums(cur)
        pv = products(1 - cur, jnp.minimum(kb + 1, i))
        for h in heads:
            run_ref[h] = weights_stage(cur, h, sums[h], run_ref[h])
        logits_stage(zs, 1 - cur, False)
        accumulate(pv)

    def two_steps(t, carry):
        step(2 * t, 0)
        step(2 * t + 1, 1)
        return carry

    logits_stage(logits(i), 0, True)
    a_ref[1] = jnp.zeros(a_ref.shape[1:], BF16)
    acc_ref[...] = jnp.zeros_like(acc_ref)
    run_ref[...] = jnp.zeros_like(run_ref)
    lax.fori_loop(0, i // 2 + 1, two_steps, 0)
    accumulate(products(1, 0))
    for c in range(pairs):
        o_t = jnp.concatenate([acc_ref[2 * c], acc_ref[2 * c + 1]], axis=0).astype(BF16)
        o_ref[0, :, tile(c)] = _mxu_transpose(o_t, eye_ref)


def _attn_sb(ya, lu, eye, tq, pairs):
    b, s, _ = ya.shape
    groups = N_PAIR_SB // pairs
    width = pairs * LANES
    heads = 2 * pairs
    return pl.pallas_call(
        functools.partial(_attn_sb_kernel, tq=tq),
        grid=(b, groups, s // tq),
        in_specs=[pl.BlockSpec((1, tq, width), lambda bi, j, i: (bi, i, j)),
                  pl.BlockSpec((1, s, width), lambda bi, j, i: (bi, 0, groups + j)),
                  pl.BlockSpec((1, s, width), lambda bi, j, i: (bi, 0, 2 * groups + j)),
                  _const_spec(lu.shape), _const_spec(eye.shape)],
        out_specs=pl.BlockSpec((1, tq, width), lambda bi, j, i: (bi, i, j)),
        out_shape=jax.ShapeDtypeStruct((b, s, W_SB), BF16),
        scratch_shapes=[pltpu.VMEM((width, s), BF16), pltpu.VMEM((heads, tq, LANES), BF16),
                        pltpu.VMEM((2, heads, tq, tq), F32), pltpu.VMEM((2, heads, tq, tq), BF16),
                        pltpu.VMEM((2, heads, tq, tq), BF16), pltpu.VMEM((heads, HEAD_DIM, tq), F32),
                        pltpu.VMEM((heads, 1, tq), F32)],
        compiler_params=_params(3),
        name="attn_stickbreak",
    )(ya, ya, ya, lu, eye)


def _softmax_scratch(n_streams, tq, acc_rows):
    return [pltpu.VMEM((2, n_streams, tq, tq), F32), pltpu.VMEM((2, n_streams, tq, tq), BF16),
            pltpu.VMEM((n_streams, acc_rows, tq), F32), pltpu.VMEM((n_streams, 1, tq), F32)]


def _softmax_walk(i, n_st, visible, score, value, s_ref, p_ref, acc_ref, stat_ref, tq):
    streams = range(n_st)
    q_tiles = range(tq // LANES)

    def step(n, cur):
        kb = i - n
        for st in streams:
            s_next = score(st, jnp.maximum(kb - 1, 0))
            s_ref[1 - cur, st] = jnp.where(kb >= 1, s_next, -jnp.inf) if cur == 0 else s_next
        for st in streams:
            alpha = []
            for j in q_tiles:
                cols = slice(j * LANES, (j + 1) * LANES)
                s = s_ref[cur, st, :, cols]
                m_old = stat_ref[st, :, cols]
                m_j = jnp.maximum(m_old, jnp.max(s, axis=0, keepdims=True))
                p_ref[cur, st, :, cols] = jnp.exp2(s - m_j).astype(BF16)
                stat_ref[st, :, cols] = m_j
                alpha.append(jnp.exp2(m_old - m_j))
            pv = _dot(value(st, jnp.maximum(kb, 0)), p_ref[cur, st])
            acc_ref[st] = acc_ref[st] * jnp.concatenate(alpha, axis=1) + pv

    def two_steps(t, carry):
        step(2 * t, 0)
        step(2 * t + 1, 1)
        return carry

    for st in streams:
        s_ref[0, st] = jnp.where(visible, score(st, i), -jnp.inf)
    acc_ref[...] = jnp.zeros_like(acc_ref)
    stat_ref[...] = jnp.full(stat_ref.shape, -jnp.inf, F32)
    lax.fori_loop(0, i // 2 + 1, two_steps, 0)
    return [acc_ref[st] for st in streams]


ONES_ROWS = 16


def _attn_fox_kernel(q_ref, k_ref, v_ref, nc_ref, eye_ref, o_ref, vt_ref, qs_ref, s_ref, p_ref, acc_ref,
                     stat_ref, *, tq):
    i = pl.program_id(2)
    tiles = q_ref.shape[2] // LANES

    def tile(c):
        return slice(c * LANES, (c + 1) * LANES)

    @pl.when(i == 0)
    def _():
        for c in range(tiles):
            def write(pos, vt, c=c):
                for h in range(2):
                    vt_ref[2 * c + h, 0:HEAD_DIM, pos] = vt[h * HEAD_DIM:(h + 1) * HEAD_DIM]

            _for_each_transposed_chunk(v_ref, tile(c), eye_ref, write)
        vt_ref[:, HEAD_DIM:, :] = jnp.ones((2 * tiles, ONES_ROWS, vt_ref.shape[2]), BF16)

    _store_split_heads(q_ref, qs_ref)

    def score(st, kb):
        bias = nc_ref[_block_ds(kb, tq), tile(st)]
        s = _dot_nt(k_ref[0, _block_ds(kb, tq), tile(st // 2)], qs_ref[st])
        return s + jnp.concatenate([bias] * (tq // LANES), axis=1)

    def value(st, kb):
        return vt_ref[st, :, _block_ds(kb, tq)]

    causal = _block_iota(tq, 0) <= _block_iota(tq, 1)
    acc = _softmax_walk(i, 2 * tiles, causal, score, value, s_ref, p_ref, acc_ref, stat_ref, tq)
    outs = [a[0:HEAD_DIM] * (1.0 / a[HEAD_DIM:HEAD_DIM + 1]) for a in acc]
    for c in range(tiles):
        o_t = jnp.concatenate(outs[2 * c:2 * c + 2], axis=0).astype(BF16)
        o_ref[0, :, tile(c)] = _mxu_transpose(o_t, eye_ref)


def _attn_fox(yb, negcum, eye, tq, tiles):
    b, s, _ = yb.shape
    groups = N_PAIR_FOX // tiles
    width = tiles * LANES
    return pl.pallas_call(
        functools.partial(_attn_fox_kernel, tq=tq),
        grid=(b, groups, s // tq),
        in_specs=[pl.BlockSpec((1, tq, width), lambda bi, j, i: (bi, i, j)),
                  pl.BlockSpec((1, s, width), lambda bi, j, i: (bi, 0, groups + j)),
                  pl.BlockSpec((1, s, width), lambda bi, j, i: (bi, 0, 2 * groups + j)),
                  pl.BlockSpec((s, 2 * width), lambda bi, j, i: (bi, j)), _const_spec(eye.shape)],
        out_specs=pl.BlockSpec((1, tq, width), lambda bi, j, i: (bi, i, j)),
        out_shape=jax.ShapeDtypeStruct((b, s, W_FOX), BF16),
        scratch_shapes=[pltpu.VMEM((2 * tiles, HEAD_DIM + ONES_ROWS, s), BF16),
                        pltpu.VMEM((2 * tiles, tq, LANES), BF16)]
        + _softmax_scratch(2 * tiles, tq, HEAD_DIM + ONES_ROWS),
        compiler_params=_params(3),
        name="attn_forget",
    )(yb, yb, yb, negcum, eye)


def _attn_diff_kernel(q_ref, k_ref, v_ref, lam_ref, subln_ref, eye_ref, o_ref, vt_ref, qs_ref, s_ref, p_ref,
                      acc_ref, stat_ref, *, tq, lam_init):
    i = pl.program_id(2)
    tiles = q_ref.shape[2] // LANES

    def tile(c):
        return slice(c * LANES, (c + 1) * LANES)

    @pl.when(i == 0)
    def _():
        for c in range(tiles):
            def write(pos, vt, c=c):
                vt_ref[c, 0:LANES, pos] = vt

            _for_each_transposed_chunk(v_ref, tile(c), eye_ref, write)
        vt_ref[:, LANES:, :] = jnp.ones((tiles, ONES_ROWS, vt_ref.shape[2]), BF16)

    _store_split_heads(q_ref, qs_ref)

    def score(st, kb):
        return _dot_nt(k_ref[0, _block_ds(kb, tq), tile(st // 2)], qs_ref[st])

    def value(st, kb):
        return vt_ref[st // 2, :, _block_ds(kb, tq)]

    visible = (_block_iota(tq, 0) // CHUNK) <= (_block_iota(tq, 1) // CHUNK)
    acc = _softmax_walk(i, 2 * tiles, visible, score, value, s_ref, p_ref, acc_ref, stat_ref, tq)

    lf = lam_ref[...]
    lam = (jnp.exp(jnp.sum(lf[0:1] * lf[1:2], axis=1, keepdims=True))
           - jnp.exp(jnp.sum(lf[2:3] * lf[3:4], axis=1, keepdims=True)) + lam_init)
    for c in range(tiles):
        a1, a2 = (a[0:LANES] * (1.0 / a[LANES:LANES + 1]) for a in acc[2 * c:2 * c + 2])
        o_t = a1 - lam * a2
        ms = jnp.mean(o_t * o_t, axis=0, keepdims=True)
        o_t = o_t * lax.rsqrt(ms + EPS) * subln_ref[...] * (1.0 - lam_init)
        o_ref[0, :, tile(c)] = _mxu_transpose(o_t.astype(BF16), eye_ref)


def _attn_diff(yc, diff_lambda, subln, eye, tq, tiles, lam_init):
    b, s, _ = yc.shape
    groups = H_DIFF // tiles
    width = tiles * LANES
    return pl.pallas_call(
        functools.partial(_attn_diff_kernel, tq=tq, lam_init=lam_init),
        grid=(b, groups, s // tq),
        in_specs=[pl.BlockSpec((1, tq, width), lambda bi, j, i: (bi, i, j)),
                  pl.BlockSpec((1, s, width), lambda bi, j, i: (bi, 0, groups + j)),
                  pl.BlockSpec((1, s, width), lambda bi, j, i: (bi, 0, 2 * groups + j)),
                  _const_spec(diff_lambda.shape), _const_spec(subln.shape), _const_spec(eye.shape)],
        out_specs=pl.BlockSpec((1, tq, width), lambda bi, j, i: (bi, i, j)),
        out_shape=jax.ShapeDtypeStruct((b, s, W_DIFF), BF16),
        scratch_shapes=[pltpu.VMEM((tiles, LANES + ONES_ROWS, s), BF16), pltpu.VMEM((2 * tiles, tq, LANES), BF16)]
        + _softmax_scratch(2 * tiles, tq, LANES + ONES_ROWS),
        compiler_params=_params(3),
        name="attn_diff",
    )(yc, yc, yc, diff_lambda, subln, eye)


def _pair_tile(x):
    return jnp.concatenate([x, x], axis=-1)


def kernel(x, p, positions, ffn1_norm, ffn1_wi, ffn1_wo, mix_norm, w_in, b_forget, qk_gain_fox, qk_gain_diff, diff_lambda, diff_subln, w_br, w_o, ffn2_norm, ffn2_wi, ffn2_wo, ple_norm, ple_gate_w, ple_proj_w):
    b, s, d = x.shape
    depth = p.shape[0]
    t = b * s
    tm = min(512, t)
    tq = min(ATTN_BLOCK, s)

    idx = jnp.arange(LANES)
    grp = jnp.where((idx[:, None] // HEAD_DIM) == (idx[None, :] // HEAD_DIM), 1.0 / HEAD_DIM, 0.0).astype(BF16)
    tri_incl = (idx[:, None] >= idx[None, :]).astype(BF16)
    eye = jnp.eye(max(tq, LANES), dtype=BF16)
    key = jnp.arange(tq)
    lu = jnp.concatenate([(key[None, :] > key[:, None]).astype(BF16), jnp.ones((ONES_ROWS, tq), BF16)], axis=0)

    rc, rs1, rs2 = _rope_tables(positions, tm)
    w_proj_all, w_gate_all = _split_w_in(w_in, min(256, d))

    h = x.reshape(t, d)
    for i in range(depth):
        lam_init = 0.8 - 0.6 * math.exp(-0.3 * i)
        w_proj, w_gate = w_proj_all[i], w_gate_all[i]
        fb_bias = jnp.repeat(b_forget[i], LANES)[None]

        h = _ffn(h, ffn1_norm[i][None], ffn1_wi[i].astype(BF16), ffn1_wo[i].astype(BF16), tm)
        ya, yb, yc, fb = _inproj(h, mix_norm[i][None], w_proj, _pair_tile(qk_gain_fox[i]),
                                 _pair_tile(qk_gain_diff[i]), grp, rc, rs1, rs2, tm)
        negcum = _forget_cumsum(fb, fb_bias, tri_incl, s)
        oa = _attn_sb(ya.reshape(b, s, -1), lu, eye, tq, SB_PAIRS)
        ob = _attn_fox(yb.reshape(b, s, -1), negcum, eye, tq, FOX_TILES)
        oc = _attn_diff(yc.reshape(b, s, -1), diff_lambda[i], diff_subln[i][:, None], eye, tq, DIFF_TILES, lam_init)
        h = _merge(h, oa.reshape(t, -1), ob.reshape(t, -1), oc.reshape(t, -1), mix_norm[i][None],
                   w_gate, w_br[i].astype(BF16), w_o[i].astype(BF16), tm)
        h = _ffn(h, ffn2_norm[i][None], ffn2_wi[i].astype(BF16), ffn2_wo[i].astype(BF16), tm)
        h = _ple(h, p[i].reshape(t, P_DIM), ple_norm[i][None], ple_gate_w[i].astype(BF16),
                 ple_proj_w[i].astype(BF16), tm)
    return h.reshape(b, s, d)
```

```python
import functools
import math

import jax
import jax.numpy as jnp
from jax import lax
from jax.experimental import pallas as pl
from jax.experimental.pallas import tpu as pltpu

D_MODEL = 1024
HEAD_DIM = 64
LANES = 128
N_PAIR_SB = 2
N_PAIR_FOX = 2
H_DIFF = 4
W_SB = 256
W_FOX = 256
W_DIFF = 512
ROT_DIM = 16
ROPE_THETA = 500000.0
D_FF = 2816
FF_CHUNK = 256
ATTN_BLOCK = 256
SB_PAIRS = 2
FOX_TILES = 2
DIFF_TILES = 4
P_DIM = 256
EPS = 1e-6
CHUNK = 64
LOG2E = math.log2(math.e)
SCALE = HEAD_DIM ** -0.5 * LOG2E
FB_COLS = 2 * N_PAIR_FOX * LANES
IN_OFFSETS = {"a": 0, "b": 768, "fb": 1536, "c": 1540, "gates": 3076, "end": 6148}
VMEM_LIMIT = 56 * 1024 * 1024

F32 = jnp.float32
BF16 = jnp.bfloat16
NT_DIMS = (((1,), (1,)), ((), ()))


def _dot(a, b):
    return jnp.dot(a, b, preferred_element_type=F32)


def _dot_nt(a, b):
    return lax.dot_general(a, b, NT_DIMS, preferred_element_type=F32)


def _rms(x, gain):
    ms = jnp.mean(x * x, axis=-1, keepdims=True)
    return x * lax.rsqrt(ms + EPS) * gain


def _split_bf16(x, parts):
    out = []
    for _ in range(parts - 1):
        bits = lax.bitcast_convert_type(x, jnp.uint32) & jnp.uint32(0xFFFF0000)
        piece = lax.bitcast_convert_type(bits, F32)
        out.append(piece.astype(BF16))
        x = x - piece
    out.append(x.astype(BF16))
    return out


def _neg_abs(x):
    bits = lax.bitcast_convert_type(x, jnp.uint32) | jnp.uint32(0x80000000)
    return lax.bitcast_convert_type(bits, F32)


def _params(n_grid):
    return pltpu.CompilerParams(
        dimension_semantics=("arbitrary",) * n_grid, vmem_limit_bytes=VMEM_LIMIT)


def _const_spec(shape):
    zeros = (0,) * len(shape)
    return pl.BlockSpec(shape, lambda *_: zeros, pipeline_mode=pl.Buffered(1))


def _ffn_kernel(h_ref, gain_ref, wi_ref, wo_ref, o_ref, acc_ref):
    h = h_ref[...]
    xn = _rms(h, gain_ref[...]).astype(BF16)
    acc_ref[...] = jnp.zeros_like(acc_ref)
    for c in range(D_FF // FF_CHUNK):
        cols = slice(c * FF_CHUNK, (c + 1) * FF_CHUNK)
        a = _dot(xn, wi_ref[:, cols])
        g = _dot(xn, wi_ref[:, D_FF + c * FF_CHUNK:D_FF + (c + 1) * FF_CHUNK])
        act = (a * jax.nn.sigmoid(a) * g).astype(BF16)
        acc_ref[...] += _dot(act, wo_ref[cols, :])
    o_ref[...] = h + 0.5 * acc_ref[...]


def _ffn(h, gain, wi, wo, tm):
    t = h.shape[0]
    row = pl.BlockSpec((tm, D_MODEL), lambda i: (i, 0))
    return pl.pallas_call(
        _ffn_kernel,
        grid=(t // tm,),
        in_specs=[row, _const_spec(gain.shape), _const_spec(wi.shape), _const_spec(wo.shape)],
        out_specs=row,
        out_shape=jax.ShapeDtypeStruct(h.shape, F32),
        scratch_shapes=[pltpu.VMEM((tm, D_MODEL), F32)],
        compiler_params=_params(1),
        name="ffn",
    )(h, gain, wi, wo)


def _split_w_in_kernel(w_ref, wp_ref, wg_ref):
    o = IN_OFFSETS
    n_fb = o["c"] - o["fb"]
    wp_ref[0, :, 0:o["fb"]] = w_ref[0, :, 0:o["fb"]].astype(BF16)
    wp_ref[0, :, o["fb"]:o["gates"] - n_fb] = w_ref[0, :, o["c"]:o["gates"]].astype(BF16)
    for head in range(n_fb):
        col = w_ref[0, :, o["fb"] + head:o["fb"] + head + 1]
        start = o["gates"] - n_fb + head * LANES
        wp_ref[0, :, start:start + LANES] = jnp.broadcast_to(col, (col.shape[0], LANES)).astype(BF16)
    wg_ref[0] = w_ref[0, :, o["gates"]:o["end"]].astype(BF16)


def _split_w_in(w_in, rows):
    depth, d, cols = w_in.shape
    o = IN_OFFSETS
    n_proj = o["gates"] - (o["c"] - o["fb"]) + FB_COLS
    n_gate = o["end"] - o["gates"]
    blk = lambda n: pl.BlockSpec((1, rows, n), lambda l, r: (l, r, 0))
    return pl.pallas_call(
        _split_w_in_kernel,
        grid=(depth, d // rows),
        in_specs=[blk(cols)],
        out_specs=[blk(n_proj), blk(n_gate)],
        out_shape=[jax.ShapeDtypeStruct((depth, d, n_proj), BF16), jax.ShapeDtypeStruct((depth, d, n_gate), BF16)],
        compiler_params=_params(2),
        name="split_w_in",
    )(w_in)


def _rope_kernel(pos_ref, const_ref, c_ref, s1_ref, s2_ref):
    ang = pos_ref[...].astype(F32) * const_ref[0:1, :]
    sin = jnp.sin(ang)
    c_ref[...] = jnp.cos(ang)
    s1_ref[...] = sin * const_ref[1:2, :]
    s2_ref[...] = sin * const_ref[2:3, :]


def _rope_tables(positions, tm):
    t = positions.size
    lane = jnp.arange(LANES) % HEAD_DIM
    inv_freq = ROPE_THETA ** (-jnp.arange(0, ROT_DIM, 2, dtype=F32) / ROT_DIM)
    half = ROT_DIM // 2
    freq_lane = jnp.where(lane < ROT_DIM, inv_freq[lane % half], 0.0)
    plus = jnp.where((lane >= half) & (lane < ROT_DIM), 1.0, 0.0)
    minus = jnp.where(lane < half, -1.0, 0.0)
    const = jnp.zeros((8, LANES), F32).at[0].set(freq_lane).at[1].set(plus).at[2].set(minus)
    out = jax.ShapeDtypeStruct((t, LANES), F32)
    row = pl.BlockSpec((tm, LANES), lambda i: (i, 0))
    return pl.pallas_call(
        _rope_kernel,
        grid=(t // tm,),
        in_specs=[pl.BlockSpec((tm, 1), lambda i: (i, 0)), _const_spec(const.shape)],
        out_specs=[row, row, row],
        out_shape=[out, out, out],
        compiler_params=_params(1),
        name="rope_tables",
    )(positions.reshape(t, 1), const)


def _inproj_kernel(h_ref, gain_ref, w_ref, gfox_ref, gdiff_ref, grp_ref, rc_ref, rs1_ref, rs2_ref,
                   ya_ref, yb_ref, yc_ref, fb_ref):
    xn = _rms(h_ref[...], gain_ref[...]).astype(BF16)
    grp = grp_ref[...]

    def head_norm(x, gain):
        ms = _dot((x * x).astype(BF16), grp)
        return x * lax.rsqrt(ms + EPS) * gain

    def rope(x):
        return (x * rc_ref[...] + pltpu.roll(x, ROT_DIM // 2, 1) * rs1_ref[...]
                + pltpu.roll(x, LANES - ROT_DIM // 2, 1) * rs2_ref[...])

    y = _dot(xn, w_ref[:, 0:768])
    ya_ref[:, 0:W_SB] = (y[:, 0:W_SB] * SCALE).astype(BF16)
    ya_ref[:, W_SB:768] = y[:, W_SB:768].astype(BF16)

    y = _dot(xn, w_ref[:, 768:1536])
    for blk in range(2 * N_PAIR_FOX):
        is_q = blk < N_PAIR_FOX
        x = head_norm(y[:, blk * LANES:(blk + 1) * LANES], gfox_ref[(0 if is_q else 1):(1 if is_q else 2), :])
        if is_q:
            x = x * SCALE
        yb_ref[:, blk * LANES:(blk + 1) * LANES] = x.astype(BF16)
    yb_ref[:, 2 * W_FOX:768] = y[:, 2 * W_FOX:768].astype(BF16)

    y = _dot(xn, w_ref[:, 1536:3072])
    for blk in range(2 * H_DIFF):
        is_q = blk < H_DIFF
        x = head_norm(y[:, blk * LANES:(blk + 1) * LANES], gdiff_ref[(0 if is_q else 1):(1 if is_q else 2), :])
        x = rope(x)
        if is_q:
            x = x * SCALE
        yc_ref[:, blk * LANES:(blk + 1) * LANES] = x.astype(BF16)
    yc_ref[:, 2 * W_DIFF:1536] = y[:, 2 * W_DIFF:1536].astype(BF16)

    fb_ref[...] = _dot(xn, w_ref[:, 3072:3072 + FB_COLS])


def _inproj(h, gain, w, gfox, gdiff, grp, rc, rs1, rs2, tm):
    t = h.shape[0]
    row = lambda n: pl.BlockSpec((tm, n), lambda i: (i, 0))
    return pl.pallas_call(
        _inproj_kernel,
        grid=(t // tm,),
        in_specs=[row(D_MODEL), _const_spec(gain.shape), _const_spec(w.shape), _const_spec(gfox.shape),
                  _const_spec(gdiff.shape), _const_spec(grp.shape), row(LANES), row(LANES), row(LANES)],
        out_specs=[row(768), row(768), row(1536), row(FB_COLS)],
        out_shape=[jax.ShapeDtypeStruct((t, 768), BF16), jax.ShapeDtypeStruct((t, 768), BF16),
                   jax.ShapeDtypeStruct((t, 1536), BF16), jax.ShapeDtypeStruct((t, FB_COLS), F32)],
        compiler_params=_params(1),
        name="inproj",
    )(h, gain, w, gfox, gdiff, grp, rc, rs1, rs2)


def _forget_kernel(fb_ref, bias_ref, tri_ref, o_ref):
    s = fb_ref.shape[0]
    tri = tri_ref[...]
    carry = jnp.zeros((1, fb_ref.shape[1]), F32)
    for blk in range(s // LANES):
        x = fb_ref[blk * LANES:(blk + 1) * LANES, :] + bias_ref[...]
        log_f = jnp.minimum(x, 0.0) - jnp.log(1.0 + jnp.exp(-jnp.abs(x)))
        cum = carry
        for piece in _split_bf16(log_f, 3):
            cum = cum + _dot(tri, piece)
        o_ref[blk * LANES:(blk + 1) * LANES, :] = cum * -LOG2E
        carry = cum[LANES - 1:LANES, :]


def _forget_cumsum(fb, bias, tri, s):
    t, cols = fb.shape
    spec = pl.BlockSpec((s, cols), lambda i: (i, 0))
    return pl.pallas_call(
        _forget_kernel,
        grid=(t // s,),
        in_specs=[spec, _const_spec(bias.shape), _const_spec(tri.shape)],
        out_specs=spec,
        out_shape=jax.ShapeDtypeStruct(fb.shape, F32),
        compiler_params=_params(1),
        name="forget_cumsum",
    )(fb, bias, tri)


def _merge_kernel(h_ref, oa_ref, ob_ref, oc_ref, gain_ref, wgate_ref, wbr_ref, wo_ref, o_ref):
    h = h_ref[...]
    u = _rms(h, gain_ref[...]).astype(BF16)
    merged = None
    row0 = 0
    for br, o_br in enumerate((oa_ref, ob_ref, oc_ref)):
        width = o_br.shape[1]
        y = _dot(o_br[...], wbr_ref[row0:row0 + width, :])
        gate = jax.nn.sigmoid(_dot(u, wgate_ref[:, br * D_MODEL:(br + 1) * D_MODEL]))
        merged = gate * y if merged is None else merged + gate * y
        row0 += width
    o_ref[...] = h + _dot(merged.astype(BF16), wo_ref[...])


def _merge(h, oa, ob, oc, gain, wgate, wbr, wo, tm):
    t = h.shape[0]
    row = lambda n: pl.BlockSpec((tm, n), lambda i: (i, 0))
    return pl.pallas_call(
        _merge_kernel,
        grid=(t // tm,),
        in_specs=[row(D_MODEL), row(W_SB), row(W_FOX), row(W_DIFF), _const_spec(gain.shape),
                  _const_spec(wgate.shape), _const_spec(wbr.shape), _const_spec(wo.shape)],
        out_specs=row(D_MODEL),
        out_shape=jax.ShapeDtypeStruct(h.shape, F32),
        compiler_params=_params(1),
        name="merge",
    )(h, oa, ob, oc, gain, wgate, wbr, wo)


def _ple_kernel(h_ref, p_ref, gain_ref, wgate_ref, wproj_ref, o_ref):
    h = h_ref[...]
    u = _rms(h, gain_ref[...]).astype(BF16)
    gate = jax.nn.sigmoid(_dot(u, wgate_ref[...]))
    o_ref[...] = h + gate * _dot(p_ref[...].astype(BF16), wproj_ref[...])


def _ple(h, p, gain, wgate, wproj, tm):
    t = h.shape[0]
    row = lambda n: pl.BlockSpec((tm, n), lambda i: (i, 0))
    return pl.pallas_call(
        _ple_kernel,
        grid=(t // tm,),
        in_specs=[row(D_MODEL), row(P_DIM), _const_spec(gain.shape), _const_spec(wgate.shape),
                  _const_spec(wproj.shape)],
        out_specs=row(D_MODEL),
        out_shape=jax.ShapeDtypeStruct(h.shape, F32),
        compiler_params=_params(1),
        name="ple",
    )(h, p, gain, wgate, wproj)


TRANSPOSE_ROWS = 512
NO_BLOCK_LOGIT = -1e30


def _low_lanes():
    return lax.broadcasted_iota(jnp.int32, (1, LANES), 1) < HEAD_DIM


def _block_iota(tq, axis):
    return lax.broadcasted_iota(jnp.int32, (tq, tq), axis)


def _block_ds(kb, tq):
    if isinstance(kb, int):
        return pl.ds(kb * tq, tq)
    return pl.ds(pl.multiple_of(kb * tq, tq), tq)


def _mxu_transpose(x, eye_ref):
    n = x.shape[1]
    return _dot_nt(eye_ref[0:n, 0:n], x).astype(BF16)


def _for_each_transposed_chunk(src_ref, lanes, eye_ref, write):
    s = src_ref.shape[1]
    step = min(TRANSPOSE_ROWS, s)
    for start in range(0, s, step):
        pos = slice(start, start + step)
        write(pos, _mxu_transpose(src_ref[0, pos, lanes], eye_ref))


def _split_heads(q):
    low = _low_lanes()
    zero = jnp.zeros_like(q)
    return jnp.where(low, q, zero), jnp.where(low, zero, q)


def _store_split_heads(q_ref, qs_ref):
    for c in range(q_ref.shape[2] // LANES):
        for h, qh in enumerate(_split_heads(q_ref[0, :, c * LANES:(c + 1) * LANES])):
            qs_ref[2 * c + h] = qh


def _attn_sb_kernel(q_ref, k_ref, v_ref, lu_ref, eye_ref, o_ref, vt_ref, qs_ref, lb_ref, lk_ref, a_ref, acc_ref,
                    run_ref, *, tq):
    i = pl.program_id(2)
    pairs = q_ref.shape[2] // LANES

    def tile(c):
        return slice(c * LANES, (c + 1) * LANES)

    @pl.when(i == 0)
    def _():
        for c in range(pairs):
            def write(pos, vt, c=c):
                vt_ref[tile(c), pos] = vt

            _for_each_transposed_chunk(v_ref, tile(c), eye_ref, write)

    _store_split_heads(q_ref, qs_ref)
    strict = _block_iota(tq, 0) < _block_iota(tq, 1)
    lu = lu_ref[...]
    heads = range(2 * pairs)

    def logits(kb):
        return [_dot_nt(k_ref[0, _block_ds(kb, tq), tile(h // 2)], qs_ref[h]) for h in heads]

    def logits_stage(zs, slot, diag):
        for h in heads:
            z = zs[h]
            log_beta = jnp.minimum(z, 0.0) - jnp.log2(1.0 + jnp.exp2(_neg_abs(z)))
            log_keep = log_beta - z
            if diag:
                log_keep = jnp.where(strict, log_keep, 0.0)
                log_beta = jnp.where(strict, log_beta, -jnp.inf)
            lb_ref[slot, h] = log_beta
            lk_ref[slot, h] = log_keep.astype(BF16)

    def suffix_sums(slot):
        return [_dot(lu, lk_ref[slot, h]) for h in heads]

    def weights_stage(slot, h, sums, run):
        a_ref[slot, h] = jnp.exp2(lb_ref[slot, h] + sums[0:tq, :] + run).astype(BF16)
        return run + sums[tq:tq + 1, :]

    def products(slot, kb):
        return [_dot(vt_ref[h * HEAD_DIM:(h + 1) * HEAD_DIM, _block_ds(kb, tq)], a_ref[slot, h]) for h in heads]

    def accumulate(pv):
        for h in heads:
            acc_ref[h] += pv[h]

    def step(n, cur):
        kb = i - n
        zs = logits(jnp.maximum(kb - 1, 0))
        if cur == 0:
            zs = [jnp.where(kb >= 1, z, NO_BLOCK_LOGIT) for z in zs]
        sums = suffix_sums(cur)
        pv = products(1 - cur, jnp.minimum(kb + 1, i))
        for h in heads:
            run_ref[h] = weights_stage(cur, h, sums[h], run_ref[h])
        logits_stage(zs, 1 - cur, False)
        accumulate(pv)

    def two_steps(t, carry):
        step(2 * t, 0)
        step(2 * t + 1, 1)
        return carry

    logits_stage(logits(i), 0, True)
    a_ref[1] = jnp.zeros(a_ref.shape[1:], BF16)
    acc_ref[...] = jnp.zeros_like(acc_ref)
    run_ref[...] = jnp.zeros_like(run_ref)
    lax.fori_loop(0, i // 2 + 1, two_steps, 0)
    accumulate(products(1, 0))
    for c in range(pairs):
        o_t = jnp.concatenate([acc_ref[2 * c], acc_ref[2 * c + 1]], axis=0).astype(BF16)
        o_ref[0, :, tile(c)] = _mxu_transpose(o_t, eye_ref)


def _attn_sb(ya, lu, eye, tq, pairs):
    b, s, _ = ya.shape
    groups = N_PAIR_SB // pairs
    width = pairs * LANES
    heads = 2 * pairs
    return pl.pallas_call(
        functools.partial(_attn_sb_kernel, tq=tq),
        grid=(b, groups, s // tq),
        in_specs=[pl.BlockSpec((1, tq, width), lambda bi, j, i: (bi, i, j)),
                  pl.BlockSpec((1, s, width), lambda bi, j, i: (bi, 0, groups + j)),
                  pl.BlockSpec((1, s, width), lambda bi, j, i: (bi, 0, 2 * groups + j)),
                  _const_spec(lu.shape), _const_spec(eye.shape)],
        out_specs=pl.BlockSpec((1, tq, width), lambda bi, j, i: (bi, i, j)),
        out_shape=jax.ShapeDtypeStruct((b, s, W_SB), BF16),
        scratch_shapes=[pltpu.VMEM((width, s), BF16), pltpu.VMEM((heads, tq, LANES), BF16),
                        pltpu.VMEM((2, heads, tq, tq), F32), pltpu.VMEM((2, heads, tq, tq), BF16),
                        pltpu.VMEM((2, heads, tq, tq), BF16), pltpu.VMEM((heads, HEAD_DIM, tq), F32),
                        pltpu.VMEM((heads, 1, tq), F32)],
        compiler_params=_params(3),
        name="attn_stickbreak",
    )(ya, ya, ya, lu, eye)


def _softmax_scratch(n_streams, tq, acc_rows):
    return [pltpu.VMEM((2, n_streams, tq, tq), F32), pltpu.VMEM((2, n_streams, tq, tq), BF16),
            pltpu.VMEM((n_streams, acc_rows, tq), F32), pltpu.VMEM((n_streams, 1, tq), F32)]


def _softmax_walk(i, n_st, visible, score, value, s_ref, p_ref, acc_ref, stat_ref, tq, scores_last):
    streams = range(n_st)
    q_tiles = range(tq // LANES)

    def step(n, cur):
        kb = i - n

        def score_next(st):
            s_next = score(st, jnp.maximum(kb - 1, 0))
            s_ref[1 - cur, st] = jnp.where(kb >= 1, s_next, -jnp.inf) if cur == 0 else s_next

        if not scores_last:
            for st in streams:
                score_next(st)
        for st in streams:
            alpha = []
            for j in q_tiles:
                cols = slice(j * LANES, (j + 1) * LANES)
                s = s_ref[cur, st, :, cols]
                m_old = stat_ref[st, :, cols]
                m_j = jnp.maximum(m_old, jnp.max(s, axis=0, keepdims=True))
                p_ref[cur, st, :, cols] = jnp.exp2(s - m_j).astype(BF16)
                stat_ref[st, :, cols] = m_j
                alpha.append(jnp.exp2(m_old - m_j))
            pv = _dot(value(st, jnp.maximum(kb, 0)), p_ref[cur, st])
            acc_ref[st] = acc_ref[st] * jnp.concatenate(alpha, axis=1) + pv
            if scores_last:
                score_next(st)

    def two_steps(t, carry):
        step(2 * t, 0)
        step(2 * t + 1, 1)
        return carry

    for st in streams:
        s_ref[0, st] = jnp.where(visible, score(st, i), -jnp.inf)
    acc_ref[...] = jnp.zeros_like(acc_ref)
    stat_ref[...] = jnp.full(stat_ref.shape, -jnp.inf, F32)
    lax.fori_loop(0, i // 2 + 1, two_steps, 0)
    return [acc_ref[st] for st in streams]


ONES_ROWS = 16


def _attn_fox_kernel(q_ref, k_ref, v_ref, nc_ref, eye_ref, o_ref, vt_ref, qs_ref, s_ref, p_ref, acc_ref,
                     stat_ref, *, tq):
    i = pl.program_id(2)
    tiles = q_ref.shape[2] // LANES

    def tile(c):
        return slice(c * LANES, (c + 1) * LANES)

    @pl.when(i == 0)
    def _():
        for c in range(tiles):
            def write(pos, vt, c=c):
                for h in range(2):
                    vt_ref[2 * c + h, 0:HEAD_DIM, pos] = vt[h * HEAD_DIM:(h + 1) * HEAD_DIM]

            _for_each_transposed_chunk(v_ref, tile(c), eye_ref, write)
        vt_ref[:, HEAD_DIM:, :] = jnp.ones((2 * tiles, ONES_ROWS, vt_ref.shape[2]), BF16)

    _store_split_heads(q_ref, qs_ref)

    def score(st, kb):
        bias = nc_ref[_block_ds(kb, tq), tile(st)]
        s = _dot_nt(k_ref[0, _block_ds(kb, tq), tile(st // 2)], qs_ref[st])
        return s + jnp.concatenate([bias] * (tq // LANES), axis=1)

    def value(st, kb):
        return vt_ref[st, :, _block_ds(kb, tq)]

    causal = _block_iota(tq, 0) <= _block_iota(tq, 1)
    acc = _softmax_walk(i, 2 * tiles, causal, score, value, s_ref, p_ref, acc_ref, stat_ref, tq, scores_last=False)
    outs = [a[0:HEAD_DIM] * (1.0 / a[HEAD_DIM:HEAD_DIM + 1]) for a in acc]
    for c in range(tiles):
        o_t = jnp.concatenate(outs[2 * c:2 * c + 2], axis=0).astype(BF16)
        o_ref[0, :, tile(c)] = _mxu_transpose(o_t, eye_ref)


def _attn_fox(yb, negcum, eye, tq, tiles):
    b, s, _ = yb.shape
    groups = N_PAIR_FOX // tiles
    width = tiles * LANES
    return pl.pallas_call(
        functools.partial(_attn_fox_kernel, tq=tq),
        grid=(b, groups, s // tq),
        in_specs=[pl.BlockSpec((1, tq, width), lambda bi, j, i: (bi, i, j)),
                  pl.BlockSpec((1, s, width), lambda bi, j, i: (bi, 0, groups + j)),
                  pl.BlockSpec((1, s, width), lambda bi, j, i: (bi, 0, 2 * groups + j)),
                  pl.BlockSpec((s, 2 * width), lambda bi, j, i: (bi, j)), _const_spec(eye.shape)],
        out_specs=pl.BlockSpec((1, tq, width), lambda bi, j, i: (bi, i, j)),
        out_shape=jax.ShapeDtypeStruct((b, s, W_FOX), BF16),
        scratch_shapes=[pltpu.VMEM((2 * tiles, HEAD_DIM + ONES_ROWS, s), BF16),
                        pltpu.VMEM((2 * tiles, tq, LANES), BF16)]
        + _softmax_scratch(2 * tiles, tq, HEAD_DIM + ONES_ROWS),
        compiler_params=_params(3),
        name="attn_forget",
    )(yb, yb, yb, negcum, eye)


def _attn_diff_kernel(q_ref, k_ref, v_ref, lam_ref, subln_ref, eye_ref, o_ref, vt_ref, qs_ref, s_ref, p_ref,
                      acc_ref, stat_ref, *, tq, lam_init):
    i = pl.program_id(2)
    tiles = q_ref.shape[2] // LANES

    def tile(c):
        return slice(c * LANES, (c + 1) * LANES)

    @pl.when(i == 0)
    def _():
        for c in range(tiles):
            def write(pos, vt, c=c):
                vt_ref[c, 0:LANES, pos] = vt

            _for_each_transposed_chunk(v_ref, tile(c), eye_ref, write)
        vt_ref[:, LANES:, :] = jnp.ones((tiles, ONES_ROWS, vt_ref.shape[2]), BF16)

    _store_split_heads(q_ref, qs_ref)

    def score(st, kb):
        return _dot_nt(k_ref[0, _block_ds(kb, tq), tile(st // 2)], qs_ref[st])

    def value(st, kb):
        return vt_ref[st // 2, :, _block_ds(kb, tq)]

    visible = (_block_iota(tq, 0) // CHUNK) <= (_block_iota(tq, 1) // CHUNK)
    acc = _softmax_walk(i, 2 * tiles, visible, score, value, s_ref, p_ref, acc_ref, stat_ref, tq, scores_last=True)

    lf = lam_ref[...]
    lam = (jnp.exp(jnp.sum(lf[0:1] * lf[1:2], axis=1, keepdims=True))
           - jnp.exp(jnp.sum(lf[2:3] * lf[3:4], axis=1, keepdims=True)) + lam_init)
    for c in range(tiles):
        a1, a2 = (a[0:LANES] * (1.0 / a[LANES:LANES + 1]) for a in acc[2 * c:2 * c + 2])
        o_t = a1 - lam * a2
        ms = jnp.mean(o_t * o_t, axis=0, keepdims=True)
        o_t = o_t * lax.rsqrt(ms + EPS) * subln_ref[...] * (1.0 - lam_init)
        o_ref[0, :, tile(c)] = _mxu_transpose(o_t.astype(BF16), eye_ref)


def _attn_diff(yc, diff_lambda, subln, eye, tq, tiles, lam_init):
    b, s, _ = yc.shape
    groups = H_DIFF // tiles
    width = tiles * LANES
    return pl.pallas_call(
        functools.partial(_attn_diff_kernel, tq=tq, lam_init=lam_init),
        grid=(b, groups, s // tq),
        in_specs=[pl.BlockSpec((1, tq, width), lambda bi, j, i: (bi, i, j)),
                  pl.BlockSpec((1, s, width), lambda bi, j, i: (bi, 0, groups + j)),
                  pl.BlockSpec((1, s, width), lambda bi, j, i: (bi, 0, 2 * groups + j)),
                  _const_spec(diff_lambda.shape), _const_spec(subln.shape), _const_spec(eye.shape)],
        out_specs=pl.BlockSpec((1, tq, width), lambda bi, j, i: (bi, i, j)),
        out_shape=jax.ShapeDtypeStruct((b, s, W_DIFF), BF16),
        scratch_shapes=[pltpu.VMEM((tiles, LANES + ONES_ROWS, s), BF16), pltpu.VMEM((2 * tiles, tq, LANES), BF16)]
        + _softmax_scratch(2 * tiles, tq, LANES + ONES_ROWS),
        compiler_params=_params(3),
        name="attn_diff",
    )(yc, yc, yc, diff_lambda, subln, eye)


def _pair_tile(x):
    return jnp.concatenate([x, x], axis=-1)


def kernel(x, p, positions, ffn1_norm, ffn1_wi, ffn1_wo, mix_norm, w_in, b_forget, qk_gain_fox, qk_gain_diff, diff_lambda, diff_subln, w_br, w_o, ffn2_norm, ffn2_wi, ffn2_wo, ple_norm, ple_gate_w, ple_proj_w):
    b, s, d = x.shape
    depth = p.shape[0]
    t = b * s
    tm = min(512, t)
    tq = min(ATTN_BLOCK, s)

    idx = jnp.arange(LANES)
    grp = jnp.where((idx[:, None] // HEAD_DIM) == (idx[None, :] // HEAD_DIM), 1.0 / HEAD_DIM, 0.0).astype(BF16)
    tri_incl = (idx[:, None] >= idx[None, :]).astype(BF16)
    eye = jnp.eye(max(tq, LANES), dtype=BF16)
    key = jnp.arange(tq)
    lu = jnp.concatenate([(key[None, :] > key[:, None]).astype(BF16), jnp.ones((ONES_ROWS, tq), BF16)], axis=0)

    rc, rs1, rs2 = _rope_tables(positions, tm)
    w_proj_all, w_gate_all = _split_w_in(w_in, min(256, d))

    h = x.reshape(t, d)
    for i in range(depth):
        lam_init = 0.8 - 0.6 * math.exp(-0.3 * i)
        w_proj, w_gate = w_proj_all[i], w_gate_all[i]
        fb_bias = jnp.repeat(b_forget[i], LANES)[None]

        h = _ffn(h, ffn1_norm[i][None], ffn1_wi[i].astype(BF16), ffn1_wo[i].astype(BF16), tm)
        ya, yb, yc, fb = _inproj(h, mix_norm[i][None], w_proj, _pair_tile(qk_gain_fox[i]),
                                 _pair_tile(qk_gain_diff[i]), grp, rc, rs1, rs2, tm)
        negcum = _forget_cumsum(fb, fb_bias, tri_incl, s)
        oa = _attn_sb(ya.reshape(b, s, -1), lu, eye, tq, SB_PAIRS)
        ob = _attn_fox(yb.reshape(b, s, -1), negcum, eye, tq, FOX_TILES)
        oc = _attn_diff(yc.reshape(b, s, -1), diff_lambda[i], diff_subln[i][:, None], eye, tq, DIFF_TILES, lam_init)
        h = _merge(h, oa.reshape(t, -1), ob.reshape(t, -1), oc.reshape(t, -1), mix_norm[i][None],
                   w_gate, w_br[i].astype(BF16), w_o[i].astype(BF16), tm)
        h = _ffn(h, ffn2_norm[i][None], ffn2_wi[i].astype(BF16), ffn2_wo[i].astype(BF16), tm)
        h = _ple(h, p[i].reshape(t, P_DIM), ple_norm[i][None], ple_gate_w[i].astype(BF16),
                 ple_proj_w[i].astype(BF16), tm)
    return h.reshape(b, s, d)
```

```python
import functools
import math

import jax
import jax.numpy as jnp
from jax import lax
from jax.experimental import pallas as pl
from jax.experimental.pallas import tpu as pltpu

D_MODEL = 1024
HEAD_DIM = 64
LANES = 128
N_PAIR_SB = 2
N_PAIR_FOX = 2
H_DIFF = 4
W_SB = 256
W_FOX = 256
W_DIFF = 512
ROT_DIM = 16
ROPE_THETA = 500000.0
D_FF = 2816
FF_CHUNK = 256
ATTN_BLOCK = 256
SB_PAIRS = 2
FOX_TILES = 2
DIFF_TILES = 4
P_DIM = 256
EPS = 1e-6
CHUNK = 64
LOG2E = math.log2(math.e)
SCALE = HEAD_DIM ** -0.5 * LOG2E
FB_COLS = 2 * N_PAIR_FOX * LANES
IN_OFFSETS = {"a": 0, "b": 768, "fb": 1536, "c": 1540, "gates": 3076, "end": 6148}
VMEM_LIMIT = 56 * 1024 * 1024

F32 = jnp.float32
BF16 = jnp.bfloat16
NT_DIMS = (((1,), (1,)), ((), ()))


def _dot(a, b):
    return jnp.dot(a, b, preferred_element_type=F32)


def _dot_nt(a, b):
    return lax.dot_general(a, b, NT_DIMS, preferred_element_type=F32)


def _rms(x, gain):
    ms = jnp.mean(x * x, axis=-1, keepdims=True)
    return x * lax.rsqrt(ms + EPS) * gain


def _split_bf16(x, parts):
    out = []
    for _ in range(parts - 1):
        bits = lax.bitcast_convert_type(x, jnp.uint32) & jnp.uint32(0xFFFF0000)
        piece = lax.bitcast_convert_type(bits, F32)
        out.append(piece.astype(BF16))
        x = x - piece
    out.append(x.astype(BF16))
    return out


def _neg_abs(x):
    bits = lax.bitcast_convert_type(x, jnp.uint32) | jnp.uint32(0x80000000)
    return lax.bitcast_convert_type(bits, F32)


def _params(n_grid):
    return pltpu.CompilerParams(
        dimension_semantics=("arbitrary",) * n_grid, vmem_limit_bytes=VMEM_LIMIT)


def _const_spec(shape):
    zeros = (0,) * len(shape)
    return pl.BlockSpec(shape, lambda *_: zeros, pipeline_mode=pl.Buffered(1))


def _ffn_block(h, gain_ref, wi_ref, wo_ref, acc_ref):
    xn = _rms(h, gain_ref[...]).astype(BF16)
    acc_ref[...] = jnp.zeros_like(acc_ref)
    for c in range(D_FF // FF_CHUNK):
        cols = slice(c * FF_CHUNK, (c + 1) * FF_CHUNK)
        a = _dot(xn, wi_ref[:, cols])
        g = _dot(xn, wi_ref[:, D_FF + c * FF_CHUNK:D_FF + (c + 1) * FF_CHUNK])
        act = (a * jax.nn.sigmoid(a) * g).astype(BF16)
        acc_ref[...] += _dot(act, wo_ref[cols, :])
    return h + 0.5 * acc_ref[...]


def _ffn_kernel(h_ref, gain_ref, wi_ref, wo_ref, o_ref, acc_ref):
    o_ref[...] = _ffn_block(h_ref[...], gain_ref, wi_ref, wo_ref, acc_ref)


def _ffn_ple_kernel(h_ref, p_ref, gain_ref, wi_ref, wo_ref, pgain_ref, wgate_ref, wproj_ref, o_ref, acc_ref):
    h = _ffn_block(h_ref[...], gain_ref, wi_ref, wo_ref, acc_ref)
    u = _rms(h, pgain_ref[...]).astype(BF16)
    gate = jax.nn.sigmoid(_dot(u, wgate_ref[...]))
    o_ref[...] = h + gate * _dot(p_ref[...].astype(BF16), wproj_ref[...])


def _ffn_ple(h, p, gain, wi, wo, pgain, wgate, wproj, tm):
    t = h.shape[0]
    row = lambda n: pl.BlockSpec((tm, n), lambda i: (i, 0))
    consts = (gain, wi, wo, pgain, wgate, wproj)
    return pl.pallas_call(
        _ffn_ple_kernel,
        grid=(t // tm,),
        in_specs=[row(D_MODEL), row(P_DIM)] + [_const_spec(c.shape) for c in consts],
        out_specs=row(D_MODEL),
        out_shape=jax.ShapeDtypeStruct(h.shape, F32),
        scratch_shapes=[pltpu.VMEM((tm, D_MODEL), F32)],
        compiler_params=_params(1),
        name="ffn_ple",
    )(h, p, *consts)


def _ffn(h, gain, wi, wo, tm):
    t = h.shape[0]
    row = pl.BlockSpec((tm, D_MODEL), lambda i: (i, 0))
    return pl.pallas_call(
        _ffn_kernel,
        grid=(t // tm,),
        in_specs=[row, _const_spec(gain.shape), _const_spec(wi.shape), _const_spec(wo.shape)],
        out_specs=row,
        out_shape=jax.ShapeDtypeStruct(h.shape, F32),
        scratch_shapes=[pltpu.VMEM((tm, D_MODEL), F32)],
        compiler_params=_params(1),
        name="ffn",
    )(h, gain, wi, wo)


def _split_w_in_kernel(w_ref, wp_ref, wg_ref):
    o = IN_OFFSETS
    n_fb = o["c"] - o["fb"]
    wp_ref[0, :, 0:o["fb"]] = w_ref[0, :, 0:o["fb"]].astype(BF16)
    wp_ref[0, :, o["fb"]:o["gates"] - n_fb] = w_ref[0, :, o["c"]:o["gates"]].astype(BF16)
    for head in range(n_fb):
        col = w_ref[0, :, o["fb"] + head:o["fb"] + head + 1]
        start = o["gates"] - n_fb + head * LANES
        wp_ref[0, :, start:start + LANES] = jnp.broadcast_to(col, (col.shape[0], LANES)).astype(BF16)
    wg_ref[0] = w_ref[0, :, o["gates"]:o["end"]].astype(BF16)


def _split_w_in(w_in, rows):
    depth, d, cols = w_in.shape
    o = IN_OFFSETS
    n_proj = o["gates"] - (o["c"] - o["fb"]) + FB_COLS
    n_gate = o["end"] - o["gates"]
    blk = lambda n: pl.BlockSpec((1, rows, n), lambda l, r: (l, r, 0))
    return pl.pallas_call(
        _split_w_in_kernel,
        grid=(depth, d // rows),
        in_specs=[blk(cols)],
        out_specs=[blk(n_proj), blk(n_gate)],
        out_shape=[jax.ShapeDtypeStruct((depth, d, n_proj), BF16), jax.ShapeDtypeStruct((depth, d, n_gate), BF16)],
        compiler_params=_params(2),
        name="split_w_in",
    )(w_in)


def _rope_kernel(pos_ref, const_ref, c_ref, s1_ref, s2_ref):
    ang = pos_ref[...].astype(F32) * const_ref[0:1, :]
    sin = jnp.sin(ang)
    c_ref[...] = jnp.cos(ang)
    s1_ref[...] = sin * const_ref[1:2, :]
    s2_ref[...] = sin * const_ref[2:3, :]


def _rope_tables(positions, tm):
    t = positions.size
    lane = jnp.arange(LANES) % HEAD_DIM
    inv_freq = ROPE_THETA ** (-jnp.arange(0, ROT_DIM, 2, dtype=F32) / ROT_DIM)
    half = ROT_DIM // 2
    freq_lane = jnp.where(lane < ROT_DIM, inv_freq[lane % half], 0.0)
    plus = jnp.where((lane >= half) & (lane < ROT_DIM), 1.0, 0.0)
    minus = jnp.where(lane < half, -1.0, 0.0)
    const = jnp.zeros((8, LANES), F32).at[0].set(freq_lane).at[1].set(plus).at[2].set(minus)
    out = jax.ShapeDtypeStruct((t, LANES), F32)
    row = pl.BlockSpec((tm, LANES), lambda i: (i, 0))
    return pl.pallas_call(
        _rope_kernel,
        grid=(t // tm,),
        in_specs=[pl.BlockSpec((tm, 1), lambda i: (i, 0)), _const_spec(const.shape)],
        out_specs=[row, row, row],
        out_shape=[out, out, out],
        compiler_params=_params(1),
        name="rope_tables",
    )(positions.reshape(t, 1), const)


def _inproj_kernel(h_ref, gain_ref, w_ref, gfox_ref, gdiff_ref, grp_ref, rc_ref, rs1_ref, rs2_ref,
                   ya_ref, yb_ref, yc_ref, fb_ref):
    xn = _rms(h_ref[...], gain_ref[...]).astype(BF16)
    grp = grp_ref[...]

    def head_norm(x, gain):
        ms = _dot((x * x).astype(BF16), grp)
        return x * lax.rsqrt(ms + EPS) * gain

    def rope(x):
        return (x * rc_ref[...] + pltpu.roll(x, ROT_DIM // 2, 1) * rs1_ref[...]
                + pltpu.roll(x, LANES - ROT_DIM // 2, 1) * rs2_ref[...])

    y = _dot(xn, w_ref[:, 0:768])
    ya_ref[:, 0:W_SB] = (y[:, 0:W_SB] * SCALE).astype(BF16)
    ya_ref[:, W_SB:768] = y[:, W_SB:768].astype(BF16)

    y = _dot(xn, w_ref[:, 768:1536])
    for blk in range(2 * N_PAIR_FOX):
        is_q = blk < N_PAIR_FOX
        x = head_norm(y[:, blk * LANES:(blk + 1) * LANES], gfox_ref[(0 if is_q else 1):(1 if is_q else 2), :])
        if is_q:
            x = x * SCALE
        yb_ref[:, blk * LANES:(blk + 1) * LANES] = x.astype(BF16)
    yb_ref[:, 2 * W_FOX:768] = y[:, 2 * W_FOX:768].astype(BF16)

    y = _dot(xn, w_ref[:, 1536:3072])
    for blk in range(2 * H_DIFF):
        is_q = blk < H_DIFF
        x = head_norm(y[:, blk * LANES:(blk + 1) * LANES], gdiff_ref[(0 if is_q else 1):(1 if is_q else 2), :])
        x = rope(x)
        if is_q:
            x = x * SCALE
        yc_ref[:, blk * LANES:(blk + 1) * LANES] = x.astype(BF16)
    yc_ref[:, 2 * W_DIFF:1536] = y[:, 2 * W_DIFF:1536].astype(BF16)

    fb_ref[...] = _dot(xn, w_ref[:, 3072:3072 + FB_COLS])


def _inproj(h, gain, w, gfox, gdiff, grp, rc, rs1, rs2, tm):
    t = h.shape[0]
    row = lambda n: pl.BlockSpec((tm, n), lambda i: (i, 0))
    return pl.pallas_call(
        _inproj_kernel,
        grid=(t // tm,),
        in_specs=[row(D_MODEL), _const_spec(gain.shape), _const_spec(w.shape), _const_spec(gfox.shape),
                  _const_spec(gdiff.shape), _const_spec(grp.shape), row(LANES), row(LANES), row(LANES)],
        out_specs=[row(768), row(768), row(1536), row(FB_COLS)],
        out_shape=[jax.ShapeDtypeStruct((t, 768), BF16), jax.ShapeDtypeStruct((t, 768), BF16),
                   jax.ShapeDtypeStruct((t, 1536), BF16), jax.ShapeDtypeStruct((t, FB_COLS), F32)],
        compiler_params=_params(1),
        name="inproj",
    )(h, gain, w, gfox, gdiff, grp, rc, rs1, rs2)


def _forget_kernel(fb_ref, bias_ref, tri_ref, o_ref):
    s = fb_ref.shape[0]
    tri = tri_ref[...]
    carry = jnp.zeros((1, fb_ref.shape[1]), F32)
    for blk in range(s // LANES):
        x = fb_ref[blk * LANES:(blk + 1) * LANES, :] + bias_ref[...]
        log_f = jnp.minimum(x, 0.0) - jnp.log(1.0 + jnp.exp(-jnp.abs(x)))
        cum = carry
        for piece in _split_bf16(log_f, 3):
            cum = cum + _dot(tri, piece)
        o_ref[blk * LANES:(blk + 1) * LANES, :] = cum * -LOG2E
        carry = cum[LANES - 1:LANES, :]


def _forget_cumsum(fb, bias, tri, s):
    t, cols = fb.shape
    spec = pl.BlockSpec((s, cols), lambda i: (i, 0))
    return pl.pallas_call(
        _forget_kernel,
        grid=(t // s,),
        in_specs=[spec, _const_spec(bias.shape), _const_spec(tri.shape)],
        out_specs=spec,
        out_shape=jax.ShapeDtypeStruct(fb.shape, F32),
        compiler_params=_params(1),
        name="forget_cumsum",
    )(fb, bias, tri)


def _merge_kernel(h_ref, oa_ref, ob_ref, oc_ref, gain_ref, wgate_ref, wbr_ref, wo_ref, o_ref):
    h = h_ref[...]
    u = _rms(h, gain_ref[...]).astype(BF16)
    merged = None
    row0 = 0
    for br, o_br in enumerate((oa_ref, ob_ref, oc_ref)):
        width = o_br.shape[1]
        y = _dot(o_br[...], wbr_ref[row0:row0 + width, :])
        gate = jax.nn.sigmoid(_dot(u, wgate_ref[:, br * D_MODEL:(br + 1) * D_MODEL]))
        merged = gate * y if merged is None else merged + gate * y
        row0 += width
    o_ref[...] = h + _dot(merged.astype(BF16), wo_ref[...])


def _merge(h, oa, ob, oc, gain, wgate, wbr, wo, tm):
    t = h.shape[0]
    row = lambda n: pl.BlockSpec((tm, n), lambda i: (i, 0))
    return pl.pallas_call(
        _merge_kernel,
        grid=(t // tm,),
        in_specs=[row(D_MODEL), row(W_SB), row(W_FOX), row(W_DIFF), _const_spec(gain.shape),
                  _const_spec(wgate.shape), _const_spec(wbr.shape), _const_spec(wo.shape)],
        out_specs=row(D_MODEL),
        out_shape=jax.ShapeDtypeStruct(h.shape, F32),
        compiler_params=_params(1),
        name="merge",
    )(h, oa, ob, oc, gain, wgate, wbr, wo)


TRANSPOSE_ROWS = 512
NO_BLOCK_LOGIT = -1e30


def _low_lanes():
    return lax.broadcasted_iota(jnp.int32, (1, LANES), 1) < HEAD_DIM


def _block_iota(tq, axis):
    return lax.broadcasted_iota(jnp.int32, (tq, tq), axis)


def _block_ds(kb, tq):
    if isinstance(kb, int):
        return pl.ds(kb * tq, tq)
    return pl.ds(pl.multiple_of(kb * tq, tq), tq)


def _mxu_transpose(x, eye_ref):
    n = x.shape[1]
    return _dot_nt(eye_ref[0:n, 0:n], x).astype(BF16)


def _for_each_transposed_chunk(src_ref, lanes, eye_ref, write):
    s = src_ref.shape[1]
    step = min(TRANSPOSE_ROWS, s)
    for start in range(0, s, step):
        pos = slice(start, start + step)
        write(pos, _mxu_transpose(src_ref[0, pos, lanes], eye_ref))


def _split_heads(q):
    low = _low_lanes()
    zero = jnp.zeros_like(q)
    return jnp.where(low, q, zero), jnp.where(low, zero, q)


def _store_split_heads(q_ref, qs_ref):
    for c in range(q_ref.shape[2] // LANES):
        for h, qh in enumerate(_split_heads(q_ref[0, :, c * LANES:(c + 1) * LANES])):
            qs_ref[2 * c + h] = qh


def _attn_sb_kernel(q_ref, k_ref, v_ref, lu_ref, eye_ref, o_ref, vt_ref, qs_ref, lb_ref, lk_ref, a_ref, acc_ref,
                    run_ref, *, tq):
    i = pl.program_id(2)
    pairs = q_ref.shape[2] // LANES

    def tile(c):
        return slice(c * LANES, (c + 1) * LANES)

    @pl.when(i == 0)
    def _():
        for c in range(pairs):
            def write(pos, vt, c=c):
                vt_ref[tile(c), pos] = vt

            _for_each_transposed_chunk(v_ref, tile(c), eye_ref, write)

    _store_split_heads(q_ref, qs_ref)
    strict = _block_iota(tq, 0) < _block_iota(tq, 1)
    lu = lu_ref[...]
    heads = range(2 * pairs)

    def logits(kb):
        return [_dot_nt(k_ref[0, _block_ds(kb, tq), tile(h // 2)], qs_ref[h]) for h in heads]

    def logits_stage(zs, slot, diag):
        for h in heads:
            z = zs[h]
            log_beta = jnp.minimum(z, 0.0) - jnp.log2(1.0 + jnp.exp2(_neg_abs(z)))
            log_keep = log_beta - z
            if diag:
                log_keep = jnp.where(strict, log_keep, 0.0)
                log_beta = jnp.where(strict, log_beta, -jnp.inf)
            lb_ref[slot, h] = log_beta
            lk_ref[slot, h] = log_keep.astype(BF16)

    def suffix_sums(slot):
        return [_dot(lu, lk_ref[slot, h]) for h in heads]

    def weights_stage(slot, h, sums, run):
        a_ref[slot, h] = jnp.exp2(lb_ref[slot, h] + sums[0:tq, :] + run).astype(BF16)
        return run + sums[tq:tq + 1, :]

    def products(slot, kb):
        return [_dot(vt_ref[h * HEAD_DIM:(h + 1) * HEAD_DIM, _block_ds(kb, tq)], a_ref[slot, h]) for h in heads]

    def accumulate(pv):
        for h in heads:
            acc_ref[h] += pv[h]

    def step(n, cur):
        kb = i - n
        zs = logits(jnp.maximum(kb - 1, 0))
        if cur == 0:
            zs = [jnp.where(kb >= 1, z, NO_BLOCK_LOGIT) for z in zs]
        sums = suffix_sums(cur)
        pv = products(1 - cur, jnp.minimum(kb + 1, i))
        for h in heads:
            run_ref[h] = weights_stage(cur, h, sums[h], run_ref[h])
        logits_stage(zs, 1 - cur, False)
        accumulate(pv)

    def two_steps(t, carry):
        step(2 * t, 0)
        step(2 * t + 1, 1)
        return carry

    logits_stage(logits(i), 0, True)
    a_ref[1] = jnp.zeros(a_ref.shape[1:], BF16)
    acc_ref[...] = jnp.zeros_like(acc_ref)
    run_ref[...] = jnp.zeros_like(run_ref)
    lax.fori_loop(0, i // 2 + 1, two_steps, 0)
    accumulate(products(1, 0))
    for c in range(pairs):
        o_t = jnp.concatenate([acc_ref[2 * c], acc_ref[2 * c + 1]], axis=0).astype(BF16)
        o_ref[0, :, tile(c)] = _mxu_transpose(o_t, eye_ref)


def _attn_sb(ya, lu, eye, tq, pairs):
    b, s, _ = ya.shape
    groups = N_PAIR_SB // pairs
    width = pairs * LANES
    heads = 2 * pairs
    return pl.pallas_call(
        functools.partial(_attn_sb_kernel, tq=tq),
        grid=(b, groups, s // tq),
        in_specs=[pl.BlockSpec((1, tq, width), lambda bi, j, i: (bi, i, j)),
                  pl.BlockSpec((1, s, width), lambda bi, j, i: (bi, 0, groups + j)),
                  pl.BlockSpec((1, s, width), lambda bi, j, i: (bi, 0, 2 * groups + j)),
                  _const_spec(lu.shape), _const_spec(eye.shape)],
        out_specs=pl.BlockSpec((1, tq, width), lambda bi, j, i: (bi, i, j)),
        out_shape=jax.ShapeDtypeStruct((b, s, W_SB), BF16),
        scratch_shapes=[pltpu.VMEM((width, s), BF16), pltpu.VMEM((heads, tq, LANES), BF16),
                        pltpu.VMEM((2, heads, tq, tq), F32), pltpu.VMEM((2, heads, tq, tq), BF16),
                        pltpu.VMEM((2, heads, tq, tq), BF16), pltpu.VMEM((heads, HEAD_DIM, tq), F32),
                        pltpu.VMEM((heads, 1, tq), F32)],
        compiler_params=_params(3),
        name="attn_stickbreak",
    )(ya, ya, ya, lu, eye)


def _softmax_scratch(n_streams, tq, acc_rows):
    return [pltpu.VMEM((2, n_streams, tq, tq), F32), pltpu.VMEM((2, n_streams, tq, tq), BF16),
            pltpu.VMEM((n_streams, acc_rows, tq), F32), pltpu.VMEM((n_streams, 1, tq), F32)]


def _softmax_walk(i, n_st, visible, score, value, s_ref, p_ref, acc_ref, stat_ref, tq, scores_last):
    streams = range(n_st)
    q_tiles = range(tq // LANES)

    def step(n, cur):
        kb = i - n

        def score_next(st):
            s_next = score(st, jnp.maximum(kb - 1, 0))
            s_ref[1 - cur, st] = jnp.where(kb >= 1, s_next, -jnp.inf) if cur == 0 else s_next

        if not scores_last:
            for st in streams:
                score_next(st)
        for st in streams:
            alpha = []
            for j in q_tiles:
                cols = slice(j * LANES, (j + 1) * LANES)
                s = s_ref[cur, st, :, cols]
                m_old = stat_ref[st, :, cols]
                m_j = jnp.maximum(m_old, jnp.max(s, axis=0, keepdims=True))
                p_ref[cur, st, :, cols] = jnp.exp2(s - m_j).astype(BF16)
                stat_ref[st, :, cols] = m_j
                alpha.append(jnp.exp2(m_old - m_j))
            pv = _dot(value(st, jnp.maximum(kb, 0)), p_ref[cur, st])
            acc_ref[st] = acc_ref[st] * jnp.concatenate(alpha, axis=1) + pv
            if scores_last:
                score_next(st)

    def two_steps(t, carry):
        step(2 * t, 0)
        step(2 * t + 1, 1)
        return carry

    for st in streams:
        s_ref[0, st] = jnp.where(visible, score(st, i), -jnp.inf)
    acc_ref[...] = jnp.zeros_like(acc_ref)
    stat_ref[...] = jnp.full(stat_ref.shape, -jnp.inf, F32)
    lax.fori_loop(0, i // 2 + 1, two_steps, 0)
    return [acc_ref[st] for st in streams]


ONES_ROWS = 16


def _attn_fox_kernel(q_ref, k_ref, v_ref, nc_ref, eye_ref, o_ref, vt_ref, qs_ref, s_ref, p_ref, acc_ref,
                     stat_ref, *, tq):
    i = pl.program_id(2)
    tiles = q_ref.shape[2] // LANES

    def tile(c):
        return slice(c * LANES, (c + 1) * LANES)

    @pl.when(i == 0)
    def _():
        for c in range(tiles):
            def write(pos, vt, c=c):
                for h in range(2):
                    vt_ref[2 * c + h, 0:HEAD_DIM, pos] = vt[h * HEAD_DIM:(h + 1) * HEAD_DIM]

            _for_each_transposed_chunk(v_ref, tile(c), eye_ref, write)
        vt_ref[:, HEAD_DIM:, :] = jnp.ones((2 * tiles, ONES_ROWS, vt_ref.shape[2]), BF16)

    _store_split_heads(q_ref, qs_ref)

    def score(st, kb):
        bias = nc_ref[_block_ds(kb, tq), tile(st)]
        s = _dot_nt(k_ref[0, _block_ds(kb, tq), tile(st // 2)], qs_ref[st])
        return s + jnp.concatenate([bias] * (tq // LANES), axis=1)

    def value(st, kb):
        return vt_ref[st, :, _block_ds(kb, tq)]

    causal = _block_iota(tq, 0) <= _block_iota(tq, 1)
    acc = _softmax_walk(i, 2 * tiles, causal, score, value, s_ref, p_ref, acc_ref, stat_ref, tq, scores_last=False)
    outs = [a[0:HEAD_DIM] * (1.0 / a[HEAD_DIM:HEAD_DIM + 1]) for a in acc]
    for c in range(tiles):
        o_t = jnp.concatenate(outs[2 * c:2 * c + 2], axis=0).astype(BF16)
        o_ref[0, :, tile(c)] = _mxu_transpose(o_t, eye_ref)


def _attn_fox(yb, negcum, eye, tq, tiles):
    b, s, _ = yb.shape
    groups = N_PAIR_FOX // tiles
    width = tiles * LANES
    return pl.pallas_call(
        functools.partial(_attn_fox_kernel, tq=tq),
        grid=(b, groups, s // tq),
        in_specs=[pl.BlockSpec((1, tq, width), lambda bi, j, i: (bi, i, j)),
                  pl.BlockSpec((1, s, width), lambda bi, j, i: (bi, 0, groups + j)),
                  pl.BlockSpec((1, s, width), lambda bi, j, i: (bi, 0, 2 * groups + j)),
                  pl.BlockSpec((s, 2 * width), lambda bi, j, i: (bi, j)), _const_spec(eye.shape)],
        out_specs=pl.BlockSpec((1, tq, width), lambda bi, j, i: (bi, i, j)),
        out_shape=jax.ShapeDtypeStruct((b, s, W_FOX), BF16),
        scratch_shapes=[pltpu.VMEM((2 * tiles, HEAD_DIM + ONES_ROWS, s), BF16),
                        pltpu.VMEM((2 * tiles, tq, LANES), BF16)]
        + _softmax_scratch(2 * tiles, tq, HEAD_DIM + ONES_ROWS),
        compiler_params=_params(3),
        name="attn_forget",
    )(yb, yb, yb, negcum, eye)


def _attn_diff_kernel(q_ref, k_ref, v_ref, lam_ref, subln_ref, eye_ref, o_ref, vt_ref, qs_ref, s_ref, p_ref,
                      acc_ref, stat_ref, *, tq, lam_init):
    i = pl.program_id(2)
    tiles = q_ref.shape[2] // LANES

    def tile(c):
        return slice(c * LANES, (c + 1) * LANES)

    @pl.when(i == 0)
    def _():
        for c in range(tiles):
            def write(pos, vt, c=c):
                vt_ref[c, 0:LANES, pos] = vt

            _for_each_transposed_chunk(v_ref, tile(c), eye_ref, write)
        vt_ref[:, LANES:, :] = jnp.ones((tiles, ONES_ROWS, vt_ref.shape[2]), BF16)

    _store_split_heads(q_ref, qs_ref)

    def score(st, kb):
        return _dot_nt(k_ref[0, _block_ds(kb, tq), tile(st // 2)], qs_ref[st])

    def value(st, kb):
        return vt_ref[st // 2, :, _block_ds(kb, tq)]

    visible = (_block_iota(tq, 0) // CHUNK) <= (_block_iota(tq, 1) // CHUNK)
    acc = _softmax_walk(i, 2 * tiles, visible, score, value, s_ref, p_ref, acc_ref, stat_ref, tq, scores_last=True)

    lf = lam_ref[...]
    lam = (jnp.exp(jnp.sum(lf[0:1] * lf[1:2], axis=1, keepdims=True))
           - jnp.exp(jnp.sum(lf[2:3] * lf[3:4], axis=1, keepdims=True)) + lam_init)
    for c in range(tiles):
        a1, a2 = (a[0:LANES] * (1.0 / a[LANES:LANES + 1]) for a in acc[2 * c:2 * c + 2])
        o_t = a1 - lam * a2
        ms = jnp.mean(o_t * o_t, axis=0, keepdims=True)
        o_t = o_t * lax.rsqrt(ms + EPS) * subln_ref[...] * (1.0 - lam_init)
        o_ref[0, :, tile(c)] = _mxu_transpose(o_t.astype(BF16), eye_ref)


def _attn_diff(yc, diff_lambda, subln, eye, tq, tiles, lam_init):
    b, s, _ = yc.shape
    groups = H_DIFF // tiles
    width = tiles * LANES
    return pl.pallas_call(
        functools.partial(_attn_diff_kernel, tq=tq, lam_init=lam_init),
        grid=(b, groups, s // tq),
        in_specs=[pl.BlockSpec((1, tq, width), lambda bi, j, i: (bi, i, j)),
                  pl.BlockSpec((1, s, width), lambda bi, j, i: (bi, 0, groups + j)),
                  pl.BlockSpec((1, s, width), lambda bi, j, i: (bi, 0, 2 * groups + j)),
                  _const_spec(diff_lambda.shape), _const_spec(subln.shape), _const_spec(eye.shape)],
        out_specs=pl.BlockSpec((1, tq, width), lambda bi, j, i: (bi, i, j)),
        out_shape=jax.ShapeDtypeStruct((b, s, W_DIFF), BF16),
        scratch_shapes=[pltpu.VMEM((tiles, LANES + ONES_ROWS, s), BF16), pltpu.VMEM((2 * tiles, tq, LANES), BF16)]
        + _softmax_scratch(2 * tiles, tq, LANES + ONES_ROWS),
        compiler_params=_params(3),
        name="attn_diff",
    )(yc, yc, yc, diff_lambda, subln, eye)


def _pair_tile(x):
    return jnp.concatenate([x, x], axis=-1)


def kernel(x, p, positions, ffn1_norm, ffn1_wi, ffn1_wo, mix_norm, w_in, b_forget, qk_gain_fox, qk_gain_diff, diff_lambda, diff_subln, w_br, w_o, ffn2_norm, ffn2_wi, ffn2_wo, ple_norm, ple_gate_w, ple_proj_w):
    b, s, d = x.shape
    depth = p.shape[0]
    t = b * s
    tm = min(512, t)
    tq = min(ATTN_BLOCK, s)

    idx = jnp.arange(LANES)
    grp = jnp.where((idx[:, None] // HEAD_DIM) == (idx[None, :] // HEAD_DIM), 1.0 / HEAD_DIM, 0.0).astype(BF16)
    tri_incl = (idx[:, None] >= idx[None, :]).astype(BF16)
    eye = jnp.eye(max(tq, LANES), dtype=BF16)
    key = jnp.arange(tq)
    lu = jnp.concatenate([(key[None, :] > key[:, None]).astype(BF16), jnp.ones((ONES_ROWS, tq), BF16)], axis=0)

    rc, rs1, rs2 = _rope_tables(positions, tm)
    w_proj_all, w_gate_all = _split_w_in(w_in, min(256, d))

    h = x.reshape(t, d)
    for i in range(depth):
        lam_init = 0.8 - 0.6 * math.exp(-0.3 * i)
        w_proj, w_gate = w_proj_all[i], w_gate_all[i]
        fb_bias = jnp.repeat(b_forget[i], LANES)[None]

        h = _ffn(h, ffn1_norm[i][None], ffn1_wi[i].astype(BF16), ffn1_wo[i].astype(BF16), tm)
        ya, yb, yc, fb = _inproj(h, mix_norm[i][None], w_proj, _pair_tile(qk_gain_fox[i]),
                                 _pair_tile(qk_gain_diff[i]), grp, rc, rs1, rs2, tm)
        negcum = _forget_cumsum(fb, fb_bias, tri_incl, s)
        oa = _attn_sb(ya.reshape(b, s, -1), lu, eye, tq, SB_PAIRS)
        ob = _attn_fox(yb.reshape(b, s, -1), negcum, eye, tq, FOX_TILES)
        oc = _attn_diff(yc.reshape(b, s, -1), diff_lambda[i], diff_subln[i][:, None], eye, tq, DIFF_TILES, lam_init)
        h = _merge(h, oa.reshape(t, -1), ob.reshape(t, -1), oc.reshape(t, -1), mix_norm[i][None],
                   w_gate, w_br[i].astype(BF16), w_o[i].astype(BF16), tm)
        h = _ffn_ple(h, p[i].reshape(t, P_DIM), ffn2_norm[i][None], ffn2_wi[i].astype(BF16),
                     ffn2_wo[i].astype(BF16), ple_norm[i][None], ple_gate_w[i].astype(BF16),
                     ple_proj_w[i].astype(BF16), tm)
    return h.reshape(b, s, d)
```
